```python
import jax, jax.numpy as jnp
from jax import lax
import numpy as np

D_MODEL = 4096
BATCH = 1
SEQ = 8192
DEPTH = 1

HEAD_DIM = 128
N_Q_HEADS = D_MODEL // HEAD_DIM
N_KV_HEADS = max(N_Q_HEADS // 4, 1)
GROUP = N_Q_HEADS // N_KV_HEADS
ATTN_WIDTH = N_Q_HEADS * HEAD_DIM
KV_WIDTH = N_KV_HEADS * HEAD_DIM
WINDOW = 128
BLOCK = 128
ROPE_THETA = 500000.0
ROT_DIM = HEAD_DIM // 4
CONV_WIDTH = D_MODEL
CONV_K = 3
RMS_EPS = 1e-6

Q_END = ATTN_WIDTH
K_END = Q_END + KV_WIDTH
V_END = K_END + KV_WIDTH
AG_END = V_END + ATTN_WIDTH
CB_END = AG_END + CONV_WIDTH
CC_END = CB_END + CONV_WIDTH
CX_END = CC_END + CONV_WIDTH
CG_END = CX_END + CONV_WIDTH
MA_END = CG_END + D_MODEL
MB_END = MA_END + D_MODEL
IN_COLS = MB_END

kernel_name = "hybrid_gated_swa_shortconv_encoder"


def rms_norm(x, gain):
    x32 = x.astype(jnp.float32)
    y = x32 * lax.rsqrt(jnp.mean(x32 * x32, axis=-1, keepdims=True) + RMS_EPS)
    return (y * gain.astype(jnp.float32)).astype(x.dtype)


def partial_rotary(t, cos, sin):
    half = ROT_DIM // 2
    t32 = t[..., :ROT_DIM].astype(jnp.float32)
    t1, t2 = t32[..., :half], t32[..., half:]
    c, s = cos[None, :, None, :], sin[None, :, None, :]
    rot = jnp.concatenate([t1 * c - t2 * s, t2 * c + t1 * s], axis=-1).astype(t.dtype)
    return jnp.concatenate([rot, t[..., ROT_DIM:]], axis=-1)


def windowed_gqa_sink(q, k, v, sink):
    b, s = q.shape[0], q.shape[1]
    nb = s // BLOCK
    qb = q.reshape(b, nb, BLOCK, N_KV_HEADS, GROUP, HEAD_DIM)

    def band(t):
        tp = jnp.pad(t, ((0, 0), (BLOCK, BLOCK), (0, 0), (0, 0)))
        tp = tp.reshape(b, nb + 2, BLOCK, N_KV_HEADS, HEAD_DIM)
        return jnp.concatenate([tp[:, :-2], tp[:, 1:-1], tp[:, 2:]], axis=2)

    kb, vb = band(k), band(v)
    scale = HEAD_DIM ** -0.5
    scores = jnp.einsum('bnqhgd,bnkhd->bnhgqk', qb, kb,
                        preferred_element_type=jnp.float32) * scale
    blk = jnp.arange(nb)[:, None, None]
    qpos = blk * BLOCK + jnp.arange(BLOCK)[None, :, None]
    kpos = (blk - 1) * BLOCK + jnp.arange(3 * BLOCK)[None, None, :]
    valid = (jnp.abs(kpos - qpos) <= WINDOW) & (kpos >= 0) & (kpos < s)
    scores = jnp.where(valid[None, :, None, None], scores, -jnp.inf)
    sink_b = sink.astype(jnp.float32).reshape(N_KV_HEADS, GROUP)[None, None, :, :, None, None]
    m = jnp.maximum(jnp.max(scores, axis=-1, keepdims=True), sink_b)
    p = jnp.exp(scores - m)
    denom = jnp.sum(p, axis=-1, keepdims=True) + jnp.exp(sink_b - m)
    probs = (p / denom).astype(v.dtype)
    out = jnp.einsum('bnhgqk,bnkhd->bnqhgd', probs, vb)
    return out.reshape(b, s, ATTN_WIDTH)


def centred_short_conv(u, w, bias):
    up = jnp.pad(u, ((0, 0), (1, 1), (0, 0)))
    return up[:, :-2] * w[0] + up[:, 1:-1] * w[1] + up[:, 2:] * w[2] + bias


def setup_inputs(seed: int = 0) -> dict:
    key = jax.random.key(seed)
    ks = jax.random.split(key, 12)
    f32 = jnp.float32
    x = jax.random.normal(ks[0], (BATCH, SEQ, D_MODEL), f32)
    norm_pre = 1.0 + 0.05 * jax.random.normal(ks[1], (DEPTH, D_MODEL), f32)
    w_in = jax.random.normal(ks[2], (DEPTH, D_MODEL, IN_COLS), f32) * D_MODEL ** -0.5
    b_merge = 0.1 * jax.random.normal(ks[3], (DEPTH, 2 * D_MODEL), f32)
    attn_sink = 0.5 * jax.random.normal(ks[4], (DEPTH, N_Q_HEADS), f32)
    conv_w = jax.random.normal(ks[5], (DEPTH, CONV_K, CONV_WIDTH), f32) * CONV_K ** -0.5
    conv_b = 0.05 * jax.random.normal(ks[6], (DEPTH, CONV_WIDTH), f32)
    w_attn_out = jax.random.normal(ks[7], (DEPTH, ATTN_WIDTH, D_MODEL), f32) * ATTN_WIDTH ** -0.5
    w_conv_out = jax.random.normal(ks[8], (DEPTH, CONV_WIDTH, D_MODEL), f32) * CONV_WIDTH ** -0.5
    w_out = jax.random.normal(ks[9], (DEPTH, D_MODEL, D_MODEL), f32) * D_MODEL ** -0.5
    norm_post = 1.0 + 0.05 * jax.random.normal(ks[10], (DEPTH, D_MODEL), f32)
    return {"x": x, "norm_pre": norm_pre, "w_in": w_in, "b_merge": b_merge,
            "attn_sink": attn_sink, "conv_w": conv_w, "conv_b": conv_b,
            "w_attn_out": w_attn_out, "w_conv_out": w_conv_out, "w_out": w_out,
            "norm_post": norm_post}


def reference(x, norm_pre, w_in, b_merge, attn_sink, conv_w, conv_b,
              w_attn_out, w_conv_out, w_out, norm_post):
    b, s, _ = x.shape
    pos = jnp.arange(s, dtype=jnp.float32)
    inv_freq = ROPE_THETA ** (-jnp.arange(0, ROT_DIM, 2, dtype=jnp.float32) / ROT_DIM)
    ang = pos[:, None] * inv_freq[None, :]
    cos, sin = jnp.cos(ang), jnp.sin(ang)

    for l in range(DEPTH):
        h = rms_norm(x, norm_pre[l])
        p = h @ w_in[l]
        q = p[..., :Q_END].reshape(b, s, N_Q_HEADS, HEAD_DIM)
        k = p[..., Q_END:K_END].reshape(b, s, N_KV_HEADS, HEAD_DIM)
        v = p[..., K_END:V_END].reshape(b, s, N_KV_HEADS, HEAD_DIM)
        attn_gate = p[..., V_END:AG_END]
        conv_bg = p[..., AG_END:CB_END]
        conv_cg = p[..., CB_END:CC_END]
        conv_x = p[..., CC_END:CX_END]
        conv_gate = p[..., CX_END:CG_END]
        merge_logits = p[..., CG_END:MB_END] + b_merge[l]

        q = partial_rotary(q, cos.astype(jnp.float32), sin.astype(jnp.float32))
        k = partial_rotary(k, cos.astype(jnp.float32), sin.astype(jnp.float32))
        a = windowed_gqa_sink(q, k, v, attn_sink[l])
        y_a = (a * jax.nn.silu(attn_gate)) @ w_attn_out[l]

        c = centred_short_conv(conv_cg * conv_x, conv_w[l], conv_b[l])
        y_b = (conv_bg * c * jax.nn.silu(conv_gate)) @ w_conv_out[l]

        g = jax.nn.sigmoid(merge_logits)
        m = g[..., :D_MODEL] * y_a + g[..., D_MODEL:] * y_b
        o = m @ w_out[l]
        x = x + rms_norm(o, norm_post[l])
    return x
```

```python
import functools
import math

import jax
import jax.numpy as jnp
from jax import lax
from jax.experimental import pallas as pl
from jax.experimental.pallas import tpu as pltpu

HEAD_DIM = 128
GROUP = 4
WINDOW = 128
BLOCK = 128
ROPE_THETA = 500000.0
ROT_DIM = HEAD_DIM // 4
ROT_HALF = ROT_DIM // 2
RMS_EPS = 1e-6
LANES = 128
V7X_VMEM_BYTES = 64 * 1024 * 1024
VMEM_LIMIT_BYTES = V7X_VMEM_BYTES - 6 * 1024 * 1024

F32 = jnp.float32
BF16 = jnp.bfloat16


def _tile(dim, pref, unit):
    t = min(pref, dim)
    t -= t % unit
    while t > unit and dim % t:
        t -= unit
    assert t >= unit and dim % t == 0, (dim, pref, unit)
    return t


def _params(*sem):
    return pltpu.CompilerParams(dimension_semantics=sem, vmem_limit_bytes=VMEM_LIMIT_BYTES)


def _rmsnorm_kernel(x_ref, g_ref, o_ref):
    x = x_ref[...]
    ms = jnp.mean(x * x, axis=-1, keepdims=True)
    o_ref[...] = (x * lax.rsqrt(ms + RMS_EPS) * g_ref[...]).astype(o_ref.dtype)


def _rmsnorm(x, gain):
    s, d = x.shape
    tm = _tile(s, 256, 8)
    return pl.pallas_call(
        _rmsnorm_kernel,
        grid=(s // tm,),
        in_specs=[pl.BlockSpec((tm, d), lambda i: (i, 0)),
                  pl.BlockSpec((1, d), lambda i: (0, 0))],
        out_specs=pl.BlockSpec((tm, d), lambda i: (i, 0)),
        out_shape=jax.ShapeDtypeStruct((s, d), BF16),
        compiler_params=_params("parallel"),
        name="rmsnorm_pre",
    )(x, gain.reshape(1, d))


def _qkv_kernel(n_rot_tiles, h_ref, w_ref, c_ref, s1_ref, s2_ref, o_ref):
    j = pl.program_id(0)
    acc = jnp.dot(h_ref[...], w_ref[...], preferred_element_type=F32)
    heads = o_ref.shape[0]

    @pl.when(j < n_rot_tiles)
    def _():
        c, s1, s2 = c_ref[...], s1_ref[...], s2_ref[...]
        for g in range(heads):
            t = acc[:, g * HEAD_DIM:(g + 1) * HEAD_DIM]
            r = (t * c + pltpu.roll(t, HEAD_DIM - ROT_HALF, 1) * s1
                 + pltpu.roll(t, ROT_HALF, 1) * s2)
            o_ref[g] = r.astype(o_ref.dtype)

    @pl.when(j >= n_rot_tiles)
    def _():
        for g in range(heads):
            o_ref[g] = acc[:, g * HEAD_DIM:(g + 1) * HEAD_DIM].astype(o_ref.dtype)


def _qkv_proj(h, w_qkv, rot_cols, tabs):
    s, d = h.shape
    n = w_qkv.shape[1]
    tn = _tile(math.gcd(n, rot_cols), 1024, HEAD_DIM)
    tm = _tile(s, 1024, 8)
    n_rot_tiles = rot_cols // tn
    heads = tn // HEAD_DIM

    def tab_map(j, i):
        return (jnp.where(j < n_rot_tiles, i, 0), 0)

    tab_spec = pl.BlockSpec((tm, HEAD_DIM), tab_map)
    return pl.pallas_call(
        functools.partial(_qkv_kernel, n_rot_tiles),
        grid=(n // tn, s // tm),
        in_specs=[pl.BlockSpec((tm, d), lambda j, i: (i, 0)),
                  pl.BlockSpec((d, tn), lambda j, i: (0, j)),
                  tab_spec, tab_spec, tab_spec],
        out_specs=pl.BlockSpec((heads, tm, HEAD_DIM), lambda j, i: (j, i, 0)),
        out_shape=jax.ShapeDtypeStruct((n // HEAD_DIM, s, HEAD_DIM), BF16),
        compiler_params=_params("parallel", "parallel"),
        name="in_proj_qkv",
    )(h, w_qkv, *tabs)


def _matmul_kernel(h_ref, w_ref, o_ref):
    o_ref[...] = jnp.dot(h_ref[...], w_ref[...], preferred_element_type=F32).astype(o_ref.dtype)


def _rest_proj(h, w_rest):
    s, d = h.shape
    n = w_rest.shape[1]
    tn = _tile(n, 1024, LANES)
    tm = _tile(s, 1024, 8)
    return pl.pallas_call(
        _matmul_kernel,
        grid=(n // tn, s // tm),
        in_specs=[pl.BlockSpec((tm, d), lambda j, i: (i, 0)),
                  pl.BlockSpec((d, tn), lambda j, i: (0, j))],
        out_specs=pl.BlockSpec((tm, tn), lambda j, i: (i, j)),
        out_shape=jax.ShapeDtypeStruct((s, n), BF16),
        compiler_params=_params("parallel", "parallel"),
        name="in_proj_rest",
    )(h, w_rest)


def _attn_kernel(nq, sink_ref, q_ref, kp_ref, km_ref, kn_ref, vp_ref, vm_ref, vn_ref,
                 gate_ref, o_ref):
    hkv = pl.program_id(0)
    t = pl.program_id(1)
    nt = pl.num_programs(1)
    scale = HEAD_DIM ** -0.5
    kfull = jnp.concatenate([kp_ref[0], km_ref[0], kn_ref[0]], axis=0)
    vfull = jnp.concatenate([vp_ref[0], vm_ref[0], vn_ref[0]], axis=0)

    rows = GROUP * BLOCK
    qi = lax.broadcasted_iota(jnp.int32, (rows, 3 * BLOCK), 0) % BLOCK
    kc = lax.broadcasted_iota(jnp.int32, (rows, 3 * BLOCK), 1)
    band = (kc >= qi) & (kc <= qi + 2 * WINDOW)
    sink_col = jnp.concatenate(
        [jnp.full((BLOCK, 1), sink_ref[hkv * GROUP + g], F32) for g in range(GROUP)], axis=0)

    for b in range(nq):
        q4 = q_ref[:, b * BLOCK:(b + 1) * BLOCK, :].reshape(rows, HEAD_DIM)
        kb = kfull[b * BLOCK:(b + 3) * BLOCK]
        vb = vfull[b * BLOCK:(b + 3) * BLOCK]
        s = lax.dot_general(q4, kb, (((1,), (1,)), ((), ())),
                            preferred_element_type=F32) * scale
        valid = band
        if b == 0:
            valid = valid & ((t > 0) | (kc >= BLOCK))
        if b == nq - 1:
            valid = valid & ((t < nt - 1) | (kc < 2 * BLOCK))
        s = jnp.where(valid, s, -jnp.inf)
        m = jnp.maximum(jnp.max(s, axis=-1, keepdims=True), sink_col)
        p = jnp.exp(s - m)
        denom = jnp.sum(p, axis=-1, keepdims=True) + jnp.exp(sink_col - m)
        probs = (p / denom).astype(vb.dtype)
        o4 = jnp.dot(probs, vb, preferred_element_type=F32)
        for g in range(GROUP):
            gate = gate_ref[b * BLOCK:(b + 1) * BLOCK, g * HEAD_DIM:(g + 1) * HEAD_DIM].astype(F32)
            o_ref[b * BLOCK:(b + 1) * BLOCK, g * HEAD_DIM:(g + 1) * HEAD_DIM] = (
                o4[g * BLOCK:(g + 1) * BLOCK] * (gate * jax.nn.sigmoid(gate))).astype(o_ref.dtype)


def _attention(qkvh, p_rest, sink, n_q_heads, n_kv_heads):
    _, s, _ = qkvh.shape
    nb = s // BLOCK
    nq = _tile(nb, 4, 1)
    tq = nq * BLOCK
    k0, v0 = n_q_heads, n_q_heads + n_kv_heads

    def main(off):
        return pl.BlockSpec((1, tq, HEAD_DIM), lambda h, t: (off + h, t, 0))

    def prev(off):
        return pl.BlockSpec((1, BLOCK, HEAD_DIM), lambda h, t: (off + h, jnp.maximum(t * nq - 1, 0), 0))

    def nxt(off):
        return pl.BlockSpec((1, BLOCK, HEAD_DIM),
                            lambda h, t: (off + h, jnp.minimum((t + 1) * nq, nb - 1), 0))

    gw = GROUP * HEAD_DIM
    return pl.pallas_call(
        functools.partial(_attn_kernel, nq),
        grid=(n_kv_heads, s // tq),
        in_specs=[pl.BlockSpec(memory_space=pltpu.SMEM),
                  pl.BlockSpec((GROUP, tq, HEAD_DIM), lambda h, t: (h, t, 0)),
                  prev(k0), main(k0), nxt(k0), prev(v0), main(v0), nxt(v0),
                  pl.BlockSpec((tq, gw), lambda h, t: (t, h))],
        out_specs=pl.BlockSpec((tq, gw), lambda h, t: (t, h)),
        out_shape=jax.ShapeDtypeStruct((s, n_q_heads * HEAD_DIM), BF16),
        compiler_params=_params("parallel", "parallel"),
        name="windowed_gqa",
    )(sink, qkvh, qkvh, qkvh, qkvh, qkvh, qkvh, qkvh, p_rest)


CONV_HALO = 16


def _conv_kernel(bg_ref, cg_ref, cx_ref, gt_ref, cgp_ref, cxp_ref, cgn_ref, cxn_ref,
                 w_ref, b_ref, o_ref):
    i = pl.program_id(0)
    ni = pl.num_programs(0)
    tm = o_ref.shape[0]
    u = cg_ref[...].astype(F32) * cx_ref[...].astype(F32)
    u_prev = (cgp_ref[CONV_HALO - 1:CONV_HALO, :].astype(F32)
              * cxp_ref[CONV_HALO - 1:CONV_HALO, :].astype(F32))
    u_next = cgn_ref[0:1, :].astype(F32) * cxn_ref[0:1, :].astype(F32)
    u_prev = jnp.where(i > 0, u_prev, 0.0)
    u_next = jnp.where(i < ni - 1, u_next, 0.0)
    row = lax.broadcasted_iota(jnp.int32, u.shape, 0)
    up = jnp.where(row == 0, u_prev, pltpu.roll(u, 1, 0))
    dn = jnp.where(row == tm - 1, u_next, pltpu.roll(u, tm - 1, 0))
    c = up * w_ref[0:1, :] + u * w_ref[1:2, :] + dn * w_ref[2:3, :] + b_ref[...]
    gate = gt_ref[...].astype(F32)
    o_ref[...] = (bg_ref[...].astype(F32) * c * (gate * jax.nn.sigmoid(gate))).astype(o_ref.dtype)


def _short_conv(p_rest, conv_w, conv_b, d, col0):
    s = p_rest.shape[0]
    tm = _tile(s, 512, CONV_HALO)
    tc = _tile(d, 1024, LANES)
    nc = d // tc
    hb = tm // CONV_HALO
    last_hb = s // CONV_HALO - 1

    def cur(k):
        return pl.BlockSpec((tm, tc), lambda i, c: (i, (col0 // tc) + k * nc + c))

    def prev(k):
        return pl.BlockSpec((CONV_HALO, tc),
                            lambda i, c: (jnp.maximum(i * hb - 1, 0), (col0 // tc) + k * nc + c))

    def nxt(k):
        return pl.BlockSpec((CONV_HALO, tc),
                            lambda i, c: (jnp.minimum((i + 1) * hb, last_hb), (col0 // tc) + k * nc + c))

    return pl.pallas_call(
        _conv_kernel,
        grid=(s // tm, nc),
        in_specs=[cur(0), cur(1), cur(2), cur(3), prev(1), prev(2), nxt(1), nxt(2),
                  pl.BlockSpec((3, tc), lambda i, c: (0, c)),
                  pl.BlockSpec((1, tc), lambda i, c: (0, c))],
        out_specs=pl.BlockSpec((tm, tc), lambda i, c: (i, c)),
        out_shape=jax.ShapeDtypeStruct((s, d), BF16),
        compiler_params=_params("parallel", "parallel"),
        name="short_conv",
    )(p_rest, p_rest, p_rest, p_rest, p_rest, p_rest, p_rest, p_rest, conv_w, conv_b.reshape(1, d))


def _merge_kernel(za_ref, zb_ref, wa_ref, wb_ref, la_ref, lb_ref, ba_ref, bb_ref, o_ref):
    ya = jnp.dot(za_ref[...], wa_ref[...], preferred_element_type=F32)
    yb = jnp.dot(zb_ref[...], wb_ref[...], preferred_element_type=F32)
    ga = jax.nn.sigmoid(la_ref[...].astype(F32) + ba_ref[...])
    gb = jax.nn.sigmoid(lb_ref[...].astype(F32) + bb_ref[...])
    o_ref[...] = (ga * ya + gb * yb).astype(o_ref.dtype)


def _out_merge(za, zb, wa, wb, p_rest, b_merge, col_a, col_b):
    s, d = za.shape
    tm = _tile(s, 512, 8)
    tn = _tile(d, 512, LANES)
    nn = d // tn
    return pl.pallas_call(
        _merge_kernel,
        grid=(s // tm, nn),
        in_specs=[pl.BlockSpec((tm, d), lambda i, j: (i, 0)),
                  pl.BlockSpec((tm, d), lambda i, j: (i, 0)),
                  pl.BlockSpec((d, tn), lambda i, j: (0, j)),
                  pl.BlockSpec((d, tn), lambda i, j: (0, j)),
                  pl.BlockSpec((tm, tn), lambda i, j: (i, col_a // tn + j)),
                  pl.BlockSpec((tm, tn), lambda i, j: (i, col_b // tn + j)),
                  pl.BlockSpec((1, tn), lambda i, j: (0, j)),
                  pl.BlockSpec((1, tn), lambda i, j: (0, nn + j))],
        out_specs=pl.BlockSpec((tm, tn), lambda i, j: (i, j)),
        out_shape=jax.ShapeDtypeStruct((s, d), BF16),
        compiler_params=_params("parallel", "parallel"),
        name="out_proj_merge",
    )(za, zb, wa, wb, p_rest, p_rest, b_merge, b_merge)


def _final_kernel(tn, m_ref, w_ref, x_ref, g_ref, o_ref, ssq_ref):
    j = pl.program_id(1)
    nj = pl.num_programs(1)
    o = jnp.dot(m_ref[...], w_ref[...], preferred_element_type=F32)

    @pl.when(j == 0)
    def _():
        ssq_ref[...] = jnp.zeros_like(ssq_ref)

    ssq_ref[...] += jnp.sum(o * o, axis=-1, keepdims=True)
    col = pl.multiple_of(j * tn, tn)
    o_ref[:, pl.ds(col, tn)] = o

    @pl.when(j == nj - 1)
    def _():
        d = o_ref.shape[1]
        rs = lax.rsqrt(ssq_ref[...] / d + RMS_EPS)
        o_ref[...] = x_ref[...] + o_ref[...] * rs * g_ref[...]


def _final(m, wo, x, gain):
    s, d = m.shape
    tm = _tile(s, 512, 8)
    tn = _tile(d, 512, LANES)
    return pl.pallas_call(
        functools.partial(_final_kernel, tn),
        grid=(s // tm, d // tn),
        in_specs=[pl.BlockSpec((tm, d), lambda i, j: (i, 0)),
                  pl.BlockSpec((d, tn), lambda i, j: (0, j)),
                  pl.BlockSpec((tm, d), lambda i, j: (i, 0)),
                  pl.BlockSpec((1, d), lambda i, j: (0, 0))],
        out_specs=pl.BlockSpec((tm, d), lambda i, j: (i, 0)),
        out_shape=jax.ShapeDtypeStruct((s, d), F32),
        scratch_shapes=[pltpu.VMEM((tm, 1), F32)],
        compiler_params=_params("parallel", "arbitrary"),
        name="wo_norm_residual",
    )(m, wo, x, gain.reshape(1, d))


def _rotary_tables(s):
    pos = jnp.arange(s, dtype=F32)
    inv_freq = ROPE_THETA ** (-jnp.arange(0, ROT_DIM, 2, dtype=F32) / ROT_DIM)
    ang = pos[:, None] * inv_freq[None, :]
    cos, sin = jnp.cos(ang), jnp.sin(ang)
    pad = jnp.zeros((s, HEAD_DIM - ROT_DIM), F32)
    zero = jnp.zeros((s, ROT_HALF), F32)
    c = jnp.concatenate([cos, cos, pad + 1.0], axis=1)
    s1 = jnp.concatenate([-sin, zero, pad], axis=1)
    s2 = jnp.concatenate([zero, sin, pad], axis=1)
    return c, s1, s2


def _layer(x, norm_pre, w_in, b_merge, sink, conv_w, conv_b, wa, wb, wo, norm_post):
    s, d = x.shape
    n_q = d // HEAD_DIM
    n_kv = max(n_q // GROUP, 1)
    attn_w, kv_w = n_q * HEAD_DIM, n_kv * HEAD_DIM
    qkv_cols = attn_w + 2 * kv_w
    conv_col0 = attn_w
    merge_a_col = attn_w + 4 * d
    merge_b_col = merge_a_col + d

    w_in16 = w_in.astype(BF16)
    h = _rmsnorm(x, norm_pre)
    qkvh = _qkv_proj(h, w_in16[:, :qkv_cols], attn_w + kv_w, _rotary_tables(s))
    p_rest = _rest_proj(h, w_in16[:, qkv_cols:])
    za = _attention(qkvh, p_rest, sink, n_q, n_kv)
    zb = _short_conv(p_rest, conv_w, conv_b, d, conv_col0)
    m = _out_merge(za, zb, wa.astype(BF16), wb.astype(BF16), p_rest,
                   b_merge.reshape(1, 2 * d), merge_a_col, merge_b_col)
    return _final(m, wo.astype(BF16), x, norm_post)


@jax.jit
def kernel(x, norm_pre, w_in, b_merge, attn_sink, conv_w, conv_b, w_attn_out, w_conv_out, w_out, norm_post):
    b, s, d = x.shape
    depth = norm_pre.shape[0]
    outs = []
    for bi in range(b):
        xb = x.reshape(s, d) if b == 1 else x[bi]
        for l in range(depth):
            xb = _layer(xb, norm_pre[l], w_in[l], b_merge[l], attn_sink[l], conv_w[l], conv_b[l],
                        w_attn_out[l], w_conv_out[l], w_out[l], norm_post[l])
        outs.append(xb)
    return outs[0].reshape(1, s, d) if b == 1 else jnp.stack(outs, axis=0)
```

```python
import functools
import math

import jax
import jax.numpy as jnp
from jax import lax
from jax.experimental import pallas as pl
from jax.experimental.pallas import tpu as pltpu

HEAD_DIM = 128
GROUP = 4
WINDOW = 128
BLOCK = 128
ROPE_THETA = 500000.0
ROT_DIM = HEAD_DIM // 4
ROT_HALF = ROT_DIM // 2
RMS_EPS = 1e-6
LOG2E = 1.4426950408889634
LANES = 128
V7X_VMEM_BYTES = 64 * 1024 * 1024
VMEM_LIMIT_BYTES = V7X_VMEM_BYTES - 6 * 1024 * 1024

F32 = jnp.float32
BF16 = jnp.bfloat16


def _tile(dim, pref, unit):
    t = min(pref, dim)
    t -= t % unit
    while t > unit and dim % t:
        t -= unit
    assert t >= unit and dim % t == 0, (dim, pref, unit)
    return t


def _params(*sem):
    return pltpu.CompilerParams(dimension_semantics=sem, vmem_limit_bytes=VMEM_LIMIT_BYTES)


def _rmsnorm_kernel(x_ref, g_ref, o_ref):
    x = x_ref[...]
    ms = jnp.mean(x * x, axis=-1, keepdims=True)
    o_ref[...] = (x * lax.rsqrt(ms + RMS_EPS) * g_ref[...]).astype(o_ref.dtype)


def _rmsnorm(x, gain):
    s, d = x.shape
    tm = _tile(s, 256, 8)
    return pl.pallas_call(
        _rmsnorm_kernel,
        grid=(s // tm,),
        in_specs=[pl.BlockSpec((tm, d), lambda i: (i, 0)),
                  pl.BlockSpec((1, d), lambda i: (0, 0))],
        out_specs=pl.BlockSpec((tm, d), lambda i: (i, 0)),
        out_shape=jax.ShapeDtypeStruct((s, d), BF16),
        compiler_params=_params("parallel"),
        name="rmsnorm_pre",
    )(x, gain.reshape(1, d))


def _qkv_kernel(h_ref, w_ref, c_ref, s1_ref, s2_ref, o_ref):
    acc = jnp.dot(h_ref[...], w_ref[...], preferred_element_type=F32)
    c, s1, s2 = c_ref[...], s1_ref[...], s2_ref[...]
    for g in range(o_ref.shape[0]):
        t = acc[:, g * HEAD_DIM:(g + 1) * HEAD_DIM]
        r = (t * c + pltpu.roll(t, HEAD_DIM - ROT_HALF, 1) * s1
             + pltpu.roll(t, ROT_HALF, 1) * s2)
        o_ref[g] = r.astype(o_ref.dtype)


def _qkv_proj(h, w_in16, qkv_cols, rot_cols):
    s, d = h.shape
    tn = _tile(math.gcd(qkv_cols, rot_cols), 1024, HEAD_DIM)
    tm = _tile(s, 1024, 8)
    n_rot_tiles = rot_cols // tn
    heads = tn // HEAD_DIM
    tabs = _rotary_tables(s, tm)
    identity_block = s // tm

    def tab_map(j, i):
        return (jnp.where(j < n_rot_tiles, i, identity_block), 0)

    tab_spec = pl.BlockSpec((tm, HEAD_DIM), tab_map)
    return pl.pallas_call(
        _qkv_kernel,
        grid=(qkv_cols // tn, s // tm),
        in_specs=[pl.BlockSpec((tm, d), lambda j, i: (i, 0)),
                  pl.BlockSpec((d, tn), lambda j, i: (0, j)),
                  tab_spec, tab_spec, tab_spec],
        out_specs=pl.BlockSpec((heads, tm, HEAD_DIM), lambda j, i: (j, i, 0)),
        out_shape=jax.ShapeDtypeStruct((qkv_cols // HEAD_DIM, s, HEAD_DIM), BF16),
        compiler_params=_params("parallel", "parallel"),
        name="in_proj_qkv",
    )(h, w_in16, *tabs)


def _matmul_kernel(h_ref, w_ref, o_ref):
    o_ref[...] = jnp.dot(h_ref[...], w_ref[...], preferred_element_type=F32).astype(o_ref.dtype)


def _rest_proj(h, w_in16, col0):
    s, d = h.shape
    n = w_in16.shape[1] - col0
    tn = _tile(math.gcd(n, col0), 1024, LANES)
    tm = _tile(s, 1024, 8)
    j0 = col0 // tn
    return pl.pallas_call(
        _matmul_kernel,
        grid=(n // tn, s // tm),
        in_specs=[pl.BlockSpec((tm, d), lambda j, i: (i, 0)),
                  pl.BlockSpec((d, tn), lambda j, i: (0, j0 + j))],
        out_specs=pl.BlockSpec((tm, tn), lambda j, i: (i, j)),
        out_shape=jax.ShapeDtypeStruct((s, n), BF16),
        compiler_params=_params("parallel", "parallel"),
        name="in_proj_rest",
    )(h, w_in16)


def _attn_kernel(nq, sink_ref, q_ref, kp_ref, km_ref, kn_ref, vp_ref, vm_ref, vn_ref,
                 gate_ref, o_ref, s_scr, p_scr):
    hkv = pl.program_id(0)
    t = pl.program_id(1)
    nt = pl.num_programs(1)
    rows = GROUP * BLOCK
    scale = HEAD_DIM ** -0.5
    kfull = jnp.concatenate([kp_ref[0], km_ref[0], kn_ref[0]], axis=0)
    vfull = jnp.concatenate([vp_ref[0], vm_ref[0], vn_ref[0]], axis=0)
    vext = jnp.concatenate([vfull, jnp.ones_like(vfull)], axis=1)

    for b in range(nq):
        q4 = q_ref[:, b * BLOCK:(b + 1) * BLOCK, :].reshape(rows, HEAD_DIM)
        s_scr[b] = lax.dot_general(q4, kfull[b * BLOCK:(b + 3) * BLOCK], (((1,), (1,)), ((), ())),
                                   preferred_element_type=F32)

    qi = lax.broadcasted_iota(jnp.int32, (rows, BLOCK), 0) % BLOCK
    kc = lax.broadcasted_iota(jnp.int32, (rows, BLOCK), 1)
    no_prev = jnp.where(t > 0, 0, BLOCK)
    no_next = jnp.where(t < nt - 1, 0, BLOCK)
    sink_col = jnp.concatenate(
        [jnp.full((BLOCK, 1), sink_ref[hkv * GROUP + g], F32) for g in range(GROUP)], axis=0)
    sink_terms = []
    for b in range(nq):
        ok_prev = kc >= (qi + no_prev if b == 0 else qi)
        ok_next = kc <= (qi - no_next if b == nq - 1 else qi)
        s0 = jnp.where(ok_prev, s_scr[b, :, 0:BLOCK], -jnp.inf)
        s1 = s_scr[b, :, BLOCK:2 * BLOCK]
        s2 = jnp.where(ok_next, s_scr[b, :, 2 * BLOCK:3 * BLOCK], -jnp.inf)
        m_raw = jnp.max(jnp.maximum(jnp.maximum(s0, s1), s2), axis=-1, keepdims=True)
        m = jnp.maximum(m_raw * scale, sink_col)
        m2 = m * LOG2E
        for k, sk in enumerate((s0, s1, s2)):
            p_scr[b, :, k * BLOCK:(k + 1) * BLOCK] = jnp.exp2(sk * (scale * LOG2E) - m2).astype(p_scr.dtype)
        sink_terms.append(jnp.exp(sink_col - m))

    for b in range(nq):
        o2 = jnp.dot(p_scr[b], vext[b * BLOCK:(b + 3) * BLOCK], preferred_element_type=F32)
        o4 = o2[:, :HEAD_DIM] / (o2[:, HEAD_DIM:] + sink_terms[b])
        for g in range(GROUP):
            gate = gate_ref[b * BLOCK:(b + 1) * BLOCK, g * HEAD_DIM:(g + 1) * HEAD_DIM].astype(F32)
            o_ref[b * BLOCK:(b + 1) * BLOCK, g * HEAD_DIM:(g + 1) * HEAD_DIM] = (
                o4[g * BLOCK:(g + 1) * BLOCK] * (gate * jax.nn.sigmoid(gate))).astype(o_ref.dtype)


def _attention(qkvh, p_rest, sink, n_q_heads, n_kv_heads):
    _, s, _ = qkvh.shape
    nb = s // BLOCK
    nq = _tile(nb, 4, 1)
    tq = nq * BLOCK
    k0, v0 = n_q_heads, n_q_heads + n_kv_heads

    def main(off):
        return pl.BlockSpec((1, tq, HEAD_DIM), lambda h, t: (off + h, t, 0))

    def prev(off):
        return pl.BlockSpec((1, BLOCK, HEAD_DIM), lambda h, t: (off + h, jnp.maximum(t * nq - 1, 0), 0))

    def nxt(off):
        return pl.BlockSpec((1, BLOCK, HEAD_DIM),
                            lambda h, t: (off + h, jnp.minimum((t + 1) * nq, nb - 1), 0))

    gw = GROUP * HEAD_DIM
    return pl.pallas_call(
        functools.partial(_attn_kernel, nq),
        grid=(n_kv_heads, s // tq),
        in_specs=[pl.BlockSpec(memory_space=pltpu.SMEM),
                  pl.BlockSpec((GROUP, tq, HEAD_DIM), lambda h, t: (h, t, 0)),
                  prev(k0), main(k0), nxt(k0), prev(v0), main(v0), nxt(v0),
                  pl.BlockSpec((tq, gw), lambda h, t: (t, h))],
        out_specs=pl.BlockSpec((tq, gw), lambda h, t: (t, h)),
        out_shape=jax.ShapeDtypeStruct((s, n_q_heads * HEAD_DIM), BF16),
        scratch_shapes=[pltpu.VMEM((nq, GROUP * BLOCK, 3 * BLOCK), F32),
                        pltpu.VMEM((nq, GROUP * BLOCK, 3 * BLOCK), BF16)],
        compiler_params=_params("parallel", "parallel"),
        name="windowed_gqa",
    )(sink, qkvh, qkvh, qkvh, qkvh, qkvh, qkvh, qkvh, p_rest)


CONV_HALO = 16


def _conv_kernel(bg_ref, cg_ref, cx_ref, gt_ref, cgp_ref, cxp_ref, cgn_ref, cxn_ref,
                 w_ref, b_ref, o_ref):
    i = pl.program_id(0)
    ni = pl.num_programs(0)
    tm = o_ref.shape[0]
    u = cg_ref[...].astype(F32) * cx_ref[...].astype(F32)
    u_prev = (cgp_ref[CONV_HALO - 1:CONV_HALO, :].astype(F32)
              * cxp_ref[CONV_HALO - 1:CONV_HALO, :].astype(F32))
    u_next = cgn_ref[0:1, :].astype(F32) * cxn_ref[0:1, :].astype(F32)
    u_prev = jnp.where(i > 0, u_prev, 0.0)
    u_next = jnp.where(i < ni - 1, u_next, 0.0)
    row = lax.broadcasted_iota(jnp.int32, u.shape, 0)
    up = jnp.where(row == 0, u_prev, pltpu.roll(u, 1, 0))
    dn = jnp.where(row == tm - 1, u_next, pltpu.roll(u, tm - 1, 0))
    c = up * w_ref[0:1, :] + u * w_ref[1:2, :] + dn * w_ref[2:3, :] + b_ref[...]
    gate = gt_ref[...].astype(F32)
    o_ref[...] = (bg_ref[...].astype(F32) * c * (gate * jax.nn.sigmoid(gate))).astype(o_ref.dtype)


def _short_conv(p_rest, conv_w, conv_b, d, col0):
    s = p_rest.shape[0]
    tm = _tile(s, 512, CONV_HALO)
    tc = _tile(d, 1024, LANES)
    nc = d // tc
    hb = tm // CONV_HALO
    last_hb = s // CONV_HALO - 1

    def cur(k):
        return pl.BlockSpec((tm, tc), lambda i, c: (i, (col0 // tc) + k * nc + c))

    def prev(k):
        return pl.BlockSpec((CONV_HALO, tc),
                            lambda i, c: (jnp.maximum(i * hb - 1, 0), (col0 // tc) + k * nc + c))

    def nxt(k):
        return pl.BlockSpec((CONV_HALO, tc),
                            lambda i, c: (jnp.minimum((i + 1) * hb, last_hb), (col0 // tc) + k * nc + c))

    return pl.pallas_call(
        _conv_kernel,
        grid=(s // tm, nc),
        in_specs=[cur(0), cur(1), cur(2), cur(3), prev(1), prev(2), nxt(1), nxt(2),
                  pl.BlockSpec((3, tc), lambda i, c: (0, c)),
                  pl.BlockSpec((1, tc), lambda i, c: (0, c))],
        out_specs=pl.BlockSpec((tm, tc), lambda i, c: (i, c)),
        out_shape=jax.ShapeDtypeStruct((s, d), BF16),
        compiler_params=_params("parallel", "parallel"),
        name="short_conv",
    )(p_rest, p_rest, p_rest, p_rest, p_rest, p_rest, p_rest, p_rest, conv_w, conv_b.reshape(1, d))


def _merge_kernel(za_ref, zb_ref, wa_ref, wb_ref, la_ref, lb_ref, ba_ref, bb_ref, o_ref):
    ya = jnp.dot(za_ref[...], wa_ref[...], preferred_element_type=F32)
    yb = jnp.dot(zb_ref[...], wb_ref[...], preferred_element_type=F32)
    ga = jax.nn.sigmoid(la_ref[...].astype(F32) + ba_ref[...])
    gb = jax.nn.sigmoid(lb_ref[...].astype(F32) + bb_ref[...])
    o_ref[...] = (ga * ya + gb * yb).astype(o_ref.dtype)


def _out_merge(za, zb, wa, wb, p_rest, b_merge, col_a, col_b):
    s, d = za.shape
    tm = _tile(s, 1024, 8)
    tn = _tile(math.gcd(d, col_a), 512, LANES)
    nn = d // tn
    return pl.pallas_call(
        _merge_kernel,
        grid=(s // tm, nn),
        in_specs=[pl.BlockSpec((tm, d), lambda i, j: (i, 0)),
                  pl.BlockSpec((tm, d), lambda i, j: (i, 0)),
                  pl.BlockSpec((d, tn), lambda i, j: (0, j)),
                  pl.BlockSpec((d, tn), lambda i, j: (0, j)),
                  pl.BlockSpec((tm, tn), lambda i, j: (i, col_a // tn + j)),
                  pl.BlockSpec((tm, tn), lambda i, j: (i, col_b // tn + j)),
                  pl.BlockSpec((1, tn), lambda i, j: (0, j)),
                  pl.BlockSpec((1, tn), lambda i, j: (0, nn + j))],
        out_specs=pl.BlockSpec((tm, tn), lambda i, j: (i, j)),
        out_shape=jax.ShapeDtypeStruct((s, d), BF16),
        compiler_params=_params("parallel", "parallel"),
        name="out_proj_merge",
    )(za, zb, wa, wb, p_rest, p_rest, b_merge, b_merge)


def _final_kernel(m_ref, w_ref, x_ref, g_ref, o_ref):
    o = jnp.dot(m_ref[...], w_ref[...], preferred_element_type=F32)
    ms = jnp.mean(o * o, axis=-1, keepdims=True)
    o_ref[...] = x_ref[...] + o * lax.rsqrt(ms + RMS_EPS) * g_ref[...]


def _final(m, wo, x, gain):
    s, d = m.shape
    tm = _tile(s, 256, 8)
    return pl.pallas_call(
        _final_kernel,
        grid=(s // tm,),
        in_specs=[pl.BlockSpec((tm, d), lambda i: (i, 0)),
                  pl.BlockSpec((d, d), lambda i: (0, 0), pipeline_mode=pl.Buffered(1)),
                  pl.BlockSpec((tm, d), lambda i: (i, 0)),
                  pl.BlockSpec((1, d), lambda i: (0, 0))],
        out_specs=pl.BlockSpec((tm, d), lambda i: (i, 0)),
        out_shape=jax.ShapeDtypeStruct((s, d), F32),
        compiler_params=_params("parallel"),
        name="wo_norm_residual",
    )(m, wo, x, gain.reshape(1, d))


def _rotary_tables(s, extra):
    pos = jnp.arange(s, dtype=F32)
    inv_freq = ROPE_THETA ** (-jnp.arange(0, ROT_DIM, 2, dtype=F32) / ROT_DIM)
    ang = pos[:, None] * inv_freq[None, :]
    cos, sin = jnp.cos(ang), jnp.sin(ang)
    pad = jnp.zeros((s, HEAD_DIM - ROT_DIM), F32)
    zero = jnp.zeros((s, ROT_HALF), F32)
    ident = jnp.zeros((extra, HEAD_DIM), F32)
    c = jnp.concatenate([jnp.concatenate([cos, cos, pad + 1.0], axis=1), ident + 1.0], axis=0)
    s1 = jnp.concatenate([jnp.concatenate([-sin, zero, pad], axis=1), ident], axis=0)
    s2 = jnp.concatenate([jnp.concatenate([zero, sin, pad], axis=1), ident], axis=0)
    return c, s1, s2


def _layer(x, norm_pre, w_in, b_merge, sink, conv_w, conv_b, wa, wb, wo, norm_post):
    s, d = x.shape
    n_q = d // HEAD_DIM
    n_kv = max(n_q // GROUP, 1)
    attn_w, kv_w = n_q * HEAD_DIM, n_kv * HEAD_DIM
    qkv_cols = attn_w + 2 * kv_w
    conv_col0 = attn_w
    merge_a_col = attn_w + 4 * d
    merge_b_col = merge_a_col + d

    w_in16 = w_in.astype(BF16)
    h = _rmsnorm(x, norm_pre)
    qkvh = _qkv_proj(h, w_in16, qkv_cols, attn_w + kv_w)
    p_rest = _rest_proj(h, w_in16, qkv_cols)
    za = _attention(qkvh, p_rest, sink, n_q, n_kv)
    zb = _short_conv(p_rest, conv_w, conv_b, d, conv_col0)
    m = _out_merge(za, zb, wa.astype(BF16), wb.astype(BF16), p_rest,
                   b_merge.reshape(1, 2 * d), merge_a_col, merge_b_col)
    return _final(m, wo.astype(BF16), x, norm_post)


@jax.jit
def kernel(x, norm_pre, w_in, b_merge, attn_sink, conv_w, conv_b, w_attn_out, w_conv_out, w_out, norm_post):
    b, s, d = x.shape
    depth = norm_pre.shape[0]
    outs = []
    for bi in range(b):
        xb = x.reshape(s, d) if b == 1 else x[bi]
        for l in range(depth):
            xb = _layer(xb, norm_pre[l], w_in[l], b_merge[l], attn_sink[l], conv_w[l], conv_b[l],
                        w_attn_out[l], w_conv_out[l], w_out[l], norm_post[l])
        outs.append(xb)
    return outs[0].reshape(1, s, d) if b == 1 else jnp.stack(outs, axis=0)
```

```python
import functools
import math

import jax
import jax.numpy as jnp
from jax import lax
from jax.experimental import pallas as pl
from jax.experimental.pallas import tpu as pltpu

HEAD_DIM = 128
GROUP = 4
WINDOW = 128
BLOCK = 128
ROPE_THETA = 500000.0
ROT_DIM = HEAD_DIM // 4
ROT_HALF = ROT_DIM // 2
RMS_EPS = 1e-6
LOG2E = 1.4426950408889634
LANES = 128
V7X_VMEM_BYTES = 64 * 1024 * 1024
VMEM_LIMIT_BYTES = V7X_VMEM_BYTES - 6 * 1024 * 1024

F32 = jnp.float32
BF16 = jnp.bfloat16


def _tile(dim, pref, unit):
    t = min(pref, dim)
    t -= t % unit
    while t > unit and dim % t:
        t -= unit
    assert t >= unit and dim % t == 0, (dim, pref, unit)
    return t


def _params(*sem):
    return pltpu.CompilerParams(dimension_semantics=sem, vmem_limit_bytes=VMEM_LIMIT_BYTES)


def _rmsnorm_kernel(x_ref, g_ref, o_ref):
    x = x_ref[...]
    ms = jnp.mean(x * x, axis=-1, keepdims=True)
    o_ref[...] = (x * lax.rsqrt(ms + RMS_EPS) * g_ref[...]).astype(o_ref.dtype)


def _rmsnorm(x, gain):
    s, d = x.shape
    tm = _tile(s, 256, 8)
    return pl.pallas_call(
        _rmsnorm_kernel,
        grid=(s // tm,),
        in_specs=[pl.BlockSpec((tm, d), lambda i: (i, 0)),
                  pl.BlockSpec((1, d), lambda i: (0, 0))],
        out_specs=pl.BlockSpec((tm, d), lambda i: (i, 0)),
        out_shape=jax.ShapeDtypeStruct((s, d), BF16),
        compiler_params=_params("parallel"),
        name="rmsnorm_pre",
    )(x, gain.reshape(1, d))


def _rotary_store(acc, tabs, o_ref):
    c, s1, s2 = (t[...] for t in tabs)
    for g in range(o_ref.shape[0]):
        t = acc[:, g * HEAD_DIM:(g + 1) * HEAD_DIM]
        r = (t * c + pltpu.roll(t, HEAD_DIM - ROT_HALF, 1) * s1
             + pltpu.roll(t, ROT_HALF, 1) * s2)
        o_ref[g] = r.astype(o_ref.dtype)


def _in_proj_kernel(n_tabs, n_riders, kc, jblk0, h_ref, w_hbm, *refs):
    tabs = refs[:n_tabs]
    riders_in = refs[n_tabs:n_tabs + n_riders]
    o_ref = refs[n_tabs + n_riders]
    riders_out = refs[n_tabs + n_riders + 1:n_tabs + 2 * n_riders + 1]
    w_even, w_odd, stage, sem = refs[n_tabs + 2 * n_riders + 1:]
    j, i = pl.program_id(0), pl.program_id(1)
    n_j, n_i = pl.num_programs(0), pl.num_programs(1)
    tn = w_even.shape[1]
    step = j * n_i + i
    slot = step % 2

    def chunk_copy(jb, ic, to_slot):
        return pltpu.make_async_copy(
            w_hbm.at[pl.ds(ic * kc, kc), pl.ds((jblk0 + jb) * tn, tn)],
            stage.at[to_slot], sem.at[to_slot])

    j_next = jnp.minimum(j + 1, n_j - 1)

    @pl.when(step == 0)
    def _():
        for ic in range(w_even.shape[0] // kc):
            cp = chunk_copy(0, ic, 0)
            cp.start()
            cp.wait()
            w_even[ic * kc:(ic + 1) * kc, :] = stage[0].astype(BF16)
        chunk_copy(j_next, 0, 0).start()

    @pl.when(step + 1 < n_j * n_i)
    def _():
        wrap = i + 1 == n_i
        chunk_copy(jnp.minimum(jnp.where(wrap, j + 2, j + 1), n_j - 1),
                   jnp.where(wrap, 0, i + 1), 1 - slot).start()

    chunk_copy(j_next, i, slot).wait()

    def compute(w_cur, w_next):
        w_next[pl.ds(pl.multiple_of(i * kc, kc), kc), :] = stage[slot].astype(BF16)
        for src, dst in zip(riders_in, riders_out):
            dst[...] = src[...].astype(dst.dtype)
        acc = jnp.dot(h_ref[...], w_cur[...], preferred_element_type=F32)
        if tabs:
            _rotary_store(acc, tabs, o_ref)
        else:
            o_ref[...] = acc.astype(o_ref.dtype)

    pl.when(j % 2 == 0)(lambda: compute(w_even, w_odd))
    pl.when(j % 2 == 1)(lambda: compute(w_odd, w_even))


def _in_proj(h, w_in, col0, n_cols, rot_cols, riders, name):
    s, d = h.shape
    tn = _tile(math.gcd(math.gcd(n_cols, col0), rot_cols), 1024, LANES)
    tm = _tile(s, 1024, 8)
    n_j, n_i = n_cols // tn, s // tm
    kc = d // n_i
    assert d % n_i == 0 and kc % 16 == 0, (d, n_i)

    in_specs = [pl.BlockSpec((tm, d), lambda j, i: (i, 0)),
                pl.BlockSpec(memory_space=pl.ANY)]
    args = [h, w_in]
    if rot_cols:
        n_rot_tiles = rot_cols // tn
        identity_block = n_i
        tab_spec = pl.BlockSpec(
            (tm, HEAD_DIM), lambda j, i: (jnp.where(j < n_rot_tiles, i, identity_block), 0))
        in_specs += [tab_spec] * 3
        args += list(_rotary_tables(s, tm))
        out_specs = [pl.BlockSpec((tn // HEAD_DIM, tm, HEAD_DIM), lambda j, i: (j, i, 0))]
        out_shape = [jax.ShapeDtypeStruct((n_cols // HEAD_DIM, s, HEAD_DIM), BF16)]
    else:
        out_specs = [pl.BlockSpec((tm, tn), lambda j, i: (i, j))]
        out_shape = [jax.ShapeDtypeStruct((s, n_cols), BF16)]

    for r in riders:
        rows, cols = r.shape
        rr = next(c for c in range(16, rows + 1, 16) if rows % c == 0 and rows // c <= n_j * n_i)
        spec = pl.BlockSpec((rr, cols), functools.partial(
            lambda j, i, last: (jnp.minimum(j * n_i + i, last), 0), last=rows // rr - 1))
        in_specs.append(spec)
        args.append(r)
        out_specs.append(spec)
        out_shape.append(jax.ShapeDtypeStruct(r.shape, BF16))

    outs = pl.pallas_call(
        functools.partial(_in_proj_kernel, 3 if rot_cols else 0, len(riders), kc, col0 // tn),
        grid=(n_j, n_i),
        in_specs=in_specs,
        out_specs=out_specs,
        out_shape=out_shape,
        scratch_shapes=[pltpu.VMEM((d, tn), BF16), pltpu.VMEM((d, tn), BF16),
                        pltpu.VMEM((2, kc, tn), F32), pltpu.SemaphoreType.DMA((2,))],
        compiler_params=_params("arbitrary", "arbitrary"),
        name=name,
    )(*args)
    return outs[0], list(outs[1:])


def _attn_kernel(nq, sink_ref, q_ref, kp_ref, km_ref, kn_ref, vp_ref, vm_ref, vn_ref,
                 gate_ref, o_ref, s_scr, p_scr):
    hkv = pl.program_id(0)
    t = pl.program_id(1)
    nt = pl.num_programs(1)
    rows = GROUP * BLOCK
    scale = HEAD_DIM ** -0.5
    kfull = jnp.concatenate([kp_ref[0], km_ref[0], kn_ref[0]], axis=0)
    vfull = jnp.concatenate([vp_ref[0], vm_ref[0], vn_ref[0]], axis=0)
    vext = jnp.concatenate([vfull, jnp.ones_like(vfull)], axis=1)

    for b in range(nq):
        q4 = q_ref[:, b * BLOCK:(b + 1) * BLOCK, :].reshape(rows, HEAD_DIM)
        s_scr[b] = lax.dot_general(q4, kfull[b * BLOCK:(b + 3) * BLOCK], (((1,), (1,)), ((), ())),
                                   preferred_element_type=F32)

    qi = lax.broadcasted_iota(jnp.int32, (rows, BLOCK), 0) % BLOCK
    kc = lax.broadcasted_iota(jnp.int32, (rows, BLOCK), 1)
    no_prev = jnp.where(t > 0, 0, BLOCK)
    no_next = jnp.where(t < nt - 1, 0, BLOCK)
    sink_col = jnp.concatenate(
        [jnp.full((BLOCK, 1), sink_ref[hkv * GROUP + g], F32) for g in range(GROUP)], axis=0)
    sink_terms = []
    for b in range(nq):
        ok_prev = kc >= (qi + no_prev if b == 0 else qi)
        ok_next = kc <= (qi - no_next if b == nq - 1 else qi)
        s0 = jnp.where(ok_prev, s_scr[b, :, 0:BLOCK], -jnp.inf)
        s1 = s_scr[b, :, BLOCK:2 * BLOCK]
        s2 = jnp.where(ok_next, s_scr[b, :, 2 * BLOCK:3 * BLOCK], -jnp.inf)
        m_raw = jnp.max(jnp.maximum(jnp.maximum(s0, s1), s2), axis=-1, keepdims=True)
        m = jnp.maximum(m_raw * scale, sink_col)
        m2 = m * LOG2E
        for k, sk in enumerate((s0, s1, s2)):
            p_scr[b, :, k * BLOCK:(k + 1) * BLOCK] = jnp.exp2(sk * (scale * LOG2E) - m2).astype(p_scr.dtype)
        sink_terms.append(jnp.exp(sink_col - m))

    for b in range(nq):
        o2 = jnp.dot(p_scr[b], vext[b * BLOCK:(b + 3) * BLOCK], preferred_element_type=F32)
        num = o2[:, :HEAD_DIM]
        den = o2[:, HEAD_DIM:] + sink_terms[b]
        for g in range(GROUP):
            gate = gate_ref[b * BLOCK:(b + 1) * BLOCK, g * HEAD_DIM:(g + 1) * HEAD_DIM].astype(F32)
            r = slice(g * BLOCK, (g + 1) * BLOCK)
            o_ref[b * BLOCK:(b + 1) * BLOCK, g * HEAD_DIM:(g + 1) * HEAD_DIM] = (
                (num[r] * gate) / (den[r] * (1.0 + jnp.exp(-gate)))).astype(o_ref.dtype)


def _attention(qkvh, p_rest, sink, n_q_heads, n_kv_heads):
    _, s, _ = qkvh.shape
    nb = s // BLOCK
    nq = _tile(nb, 4, 1)
    tq = nq * BLOCK
    k0, v0 = n_q_heads, n_q_heads + n_kv_heads

    def main(off):
        return pl.BlockSpec((1, tq, HEAD_DIM), lambda h, t: (off + h, t, 0))

    def prev(off):
        return pl.BlockSpec((1, BLOCK, HEAD_DIM), lambda h, t: (off + h, jnp.maximum(t * nq - 1, 0), 0))

    def nxt(off):
        return pl.BlockSpec((1, BLOCK, HEAD_DIM),
                            lambda h, t: (off + h, jnp.minimum((t + 1) * nq, nb - 1), 0))

    gw = GROUP * HEAD_DIM
    return pl.pallas_call(
        functools.partial(_attn_kernel, nq),
        grid=(n_kv_heads, s // tq),
        in_specs=[pl.BlockSpec(memory_space=pltpu.SMEM),
                  pl.BlockSpec((GROUP, tq, HEAD_DIM), lambda h, t: (h, t, 0)),
                  prev(k0), main(k0), nxt(k0), prev(v0), main(v0), nxt(v0),
                  pl.BlockSpec((tq, gw), lambda h, t: (t, h))],
        out_specs=pl.BlockSpec((tq, gw), lambda h, t: (t, h)),
        out_shape=jax.ShapeDtypeStruct((s, n_q_heads * HEAD_DIM), BF16),
        scratch_shapes=[pltpu.VMEM((nq, GROUP * BLOCK, 3 * BLOCK), F32),
                        pltpu.VMEM((nq, GROUP * BLOCK, 3 * BLOCK), BF16)],
        compiler_params=_params("parallel", "parallel"),
        name="windowed_gqa",
    )(sink, qkvh, qkvh, qkvh, qkvh, qkvh, qkvh, qkvh, p_rest)


CONV_HALO = 16


def _conv_kernel(bg_ref, cg_ref, cx_ref, gt_ref, cgp_ref, cxp_ref, cgn_ref, cxn_ref,
                 w_ref, b_ref, o_ref):
    i = pl.program_id(0)
    ni = pl.num_programs(0)
    tm = o_ref.shape[0]
    u = cg_ref[...].astype(F32) * cx_ref[...].astype(F32)
    u_prev = (cgp_ref[CONV_HALO - 1:CONV_HALO, :].astype(F32)
              * cxp_ref[CONV_HALO - 1:CONV_HALO, :].astype(F32))
    u_next = cgn_ref[0:1, :].astype(F32) * cxn_ref[0:1, :].astype(F32)
    u_prev = jnp.where(i > 0, u_prev, 0.0)
    u_next = jnp.where(i < ni - 1, u_next, 0.0)
    row = lax.broadcasted_iota(jnp.int32, u.shape, 0)
    up = jnp.where(row == 0, u_prev, pltpu.roll(u, 1, 0))
    dn = jnp.where(row == tm - 1, u_next, pltpu.roll(u, tm - 1, 0))
    c = up * w_ref[0:1, :] + u * w_ref[1:2, :] + dn * w_ref[2:3, :] + b_ref[...]
    gate = gt_ref[...].astype(F32)
    o_ref[...] = (bg_ref[...].astype(F32) * c * (gate * jax.nn.sigmoid(gate))).astype(o_ref.dtype)


def _short_conv(p_rest, conv_w, conv_b, d, col0):
    s = p_rest.shape[0]
    tm = _tile(s, 512, CONV_HALO)
    tc = _tile(d, 1024, LANES)
    nc = d // tc
    hb = tm // CONV_HALO
    last_hb = s // CONV_HALO - 1

    def cur(k):
        return pl.BlockSpec((tm, tc), lambda i, c: (i, (col0 // tc) + k * nc + c))

    def prev(k):
        return pl.BlockSpec((CONV_HALO, tc),
                            lambda i, c: (jnp.maximum(i * hb - 1, 0), (col0 // tc) + k * nc + c))

    def nxt(k):
        return pl.BlockSpec((CONV_HALO, tc),
                            lambda i, c: (jnp.minimum((i + 1) * hb, last_hb), (col0 // tc) + k * nc + c))

    return pl.pallas_call(
        _conv_kernel,
        grid=(s // tm, nc),
        in_specs=[cur(0), cur(1), cur(2), cur(3), prev(1), prev(2), nxt(1), nxt(2),
                  pl.BlockSpec((3, tc), lambda i, c: (0, c)),
                  pl.BlockSpec((1, tc), lambda i, c: (0, c))],
        out_specs=pl.BlockSpec((tm, tc), lambda i, c: (i, c)),
        out_shape=jax.ShapeDtypeStruct((s, d), BF16),
        compiler_params=_params("parallel", "parallel"),
        name="short_conv",
    )(p_rest, p_rest, p_rest, p_rest, p_rest, p_rest, p_rest, p_rest, conv_w, conv_b.reshape(1, d))


def _merge_kernel(za_ref, zb_ref, wa_ref, wb_ref, la_ref, lb_ref, ba_ref, bb_ref, o_ref):
    ya = jnp.dot(za_ref[...], wa_ref[...], preferred_element_type=F32)
    yb = jnp.dot(zb_ref[...], wb_ref[...], preferred_element_type=F32)
    ga = jax.nn.sigmoid(la_ref[...].astype(F32) + ba_ref[...])
    gb = jax.nn.sigmoid(lb_ref[...].astype(F32) + bb_ref[...])
    o_ref[...] = (ga * ya + gb * yb).astype(o_ref.dtype)


def _out_merge(za, zb, wa, wb, p_rest, b_merge, col_a, col_b):
    s, d = za.shape
    tm = _tile(s, 1024, 8)
    tn = _tile(math.gcd(d, col_a), 512, LANES)
    nn = d // tn
    return pl.pallas_call(
        _merge_kernel,
        grid=(s // tm, nn),
        in_specs=[pl.BlockSpec((tm, d), lambda i, j: (i, 0)),
                  pl.BlockSpec((tm, d), lambda i, j: (i, 0)),
                  pl.BlockSpec((d, tn), lambda i, j: (0, j)),
                  pl.BlockSpec((d, tn), lambda i, j: (0, j)),
                  pl.BlockSpec((tm, tn), lambda i, j: (i, col_a // tn + j)),
                  pl.BlockSpec((tm, tn), lambda i, j: (i, col_b // tn + j)),
                  pl.BlockSpec((1, tn), lambda i, j: (0, j)),
                  pl.BlockSpec((1, tn), lambda i, j: (0, nn + j))],
        out_specs=pl.BlockSpec((tm, tn), lambda i, j: (i, j)),
        out_shape=jax.ShapeDtypeStruct((s, d), BF16),
        compiler_params=_params("parallel", "parallel"),
        name="out_proj_merge",
    )(za, zb, wa, wb, p_rest, p_rest, b_merge, b_merge)


def _final_kernel(m_ref, w_ref, x_ref, g_ref, o_ref):
    o = jnp.dot(m_ref[...], w_ref[...], preferred_element_type=F32)
    ms = jnp.mean(o * o, axis=-1, keepdims=True)
    o_ref[...] = x_ref[...] + o * lax.rsqrt(ms + RMS_EPS) * g_ref[...]


def _final(m, wo, x, gain):
    s, d = m.shape
    tm = _tile(s, 256, 8)
    return pl.pallas_call(
        _final_kernel,
        grid=(s // tm,),
        in_specs=[pl.BlockSpec((tm, d), lambda i: (i, 0)),
                  pl.BlockSpec((d, d), lambda i: (0, 0), pipeline_mode=pl.Buffered(1)),
                  pl.BlockSpec((tm, d), lambda i: (i, 0)),
                  pl.BlockSpec((1, d), lambda i: (0, 0))],
        out_specs=pl.BlockSpec((tm, d), lambda i: (i, 0)),
        out_shape=jax.ShapeDtypeStruct((s, d), F32),
        compiler_params=_params("parallel"),
        name="wo_norm_residual",
    )(m, wo, x, gain.reshape(1, d))


def _rotary_tables(s, extra):
    pos = jnp.arange(s, dtype=F32)
    inv_freq = ROPE_THETA ** (-jnp.arange(0, ROT_DIM, 2, dtype=F32) / ROT_DIM)
    ang = pos[:, None] * inv_freq[None, :]
    cos, sin = jnp.cos(ang), jnp.sin(ang)
    pad = jnp.zeros((s, HEAD_DIM - ROT_DIM), F32)
    zero = jnp.zeros((s, ROT_HALF), F32)
    ident = jnp.zeros((extra, HEAD_DIM), F32)
    c = jnp.concatenate([jnp.concatenate([cos, cos, pad + 1.0], axis=1), ident + 1.0], axis=0)
    s1 = jnp.concatenate([jnp.concatenate([-sin, zero, pad], axis=1), ident], axis=0)
    s2 = jnp.concatenate([jnp.concatenate([zero, sin, pad], axis=1), ident], axis=0)
    return c, s1, s2


def _layer(x, norm_pre, w_in, b_merge, sink, conv_w, conv_b, wa, wb, wo, norm_post):
    s, d = x.shape
    n_q = d // HEAD_DIM
    n_kv = max(n_q // GROUP, 1)
    attn_w, kv_w = n_q * HEAD_DIM, n_kv * HEAD_DIM
    qkv_cols = attn_w + 2 * kv_w
    conv_col0 = attn_w
    merge_a_col = attn_w + 4 * d
    merge_b_col = merge_a_col + d

    h = _rmsnorm(x, norm_pre)
    qkvh, _ = _in_proj(h, w_in, 0, qkv_cols, attn_w + kv_w, [], "in_proj_qkv")
    p_rest, (wa16, wb16, wo16) = _in_proj(h, w_in, qkv_cols, w_in.shape[1] - qkv_cols, 0,
                                          [wa, wb, wo], "in_proj_rest")
    za = _attention(qkvh, p_rest, sink, n_q, n_kv)
    zb = _short_conv(p_rest, conv_w, conv_b, d, conv_col0)
    m = _out_merge(za, zb, wa16, wb16, p_rest, b_merge.reshape(1, 2 * d), merge_a_col, merge_b_col)
    return _final(m, wo16, x, norm_post)


@jax.jit
def kernel(x, norm_pre, w_in, b_merge, attn_sink, conv_w, conv_b, w_attn_out, w_conv_out, w_out, norm_post):
    b, s, d = x.shape
    depth = norm_pre.shape[0]
    outs = []
    for bi in range(b):
        xb = x.reshape(s, d) if b == 1 else x[bi]
        for l in range(depth):
            xb = _layer(xb, norm_pre[l], w_in[l], b_merge[l], attn_sink[l], conv_w[l], conv_b[l],
                        w_attn_out[l], w_conv_out[l], w_out[l], norm_post[l])
        outs.append(xb)
    return outs[0].reshape(1, s, d) if b == 1 else jnp.stack(outs, axis=0)
```

```python
import functools
import math
from typing import NamedTuple, Optional

import jax
import jax.numpy as jnp
from jax import lax
from jax.experimental import pallas as pl
from jax.experimental.pallas import tpu as pltpu

HEAD_DIM = 128
GROUP = 4
WINDOW = 128
BLOCK = 128
ROPE_THETA = 500000.0
ROT_DIM = HEAD_DIM // 4
ROT_HALF = ROT_DIM // 2
RMS_EPS = 1e-6
LOG2E = 1.4426950408889634
LANES = 128
BF16_SUBLANES = 16
V7X_VMEM_BYTES = 64 * 1024 * 1024
VMEM_LIMIT_BYTES = V7X_VMEM_BYTES - 6 * 1024 * 1024

F32 = jnp.float32
BF16 = jnp.bfloat16


def _tile(dim, pref, unit):
    t = min(pref, dim)
    t -= t % unit
    while t > unit and dim % t:
        t -= unit
    assert t >= unit and dim % t == 0, (dim, pref, unit)
    return t


def _params(*sem):
    return pltpu.CompilerParams(dimension_semantics=sem, vmem_limit_bytes=VMEM_LIMIT_BYTES)


def _rmsnorm_kernel(x_ref, g_ref, o_ref):
    x = x_ref[...]
    ms = jnp.mean(x * x, axis=-1, keepdims=True)
    o_ref[...] = (x * lax.rsqrt(ms + RMS_EPS) * g_ref[...]).astype(o_ref.dtype)


def _rmsnorm(x, gain):
    s, d = x.shape
    tm = _tile(s, 256, 8)
    return pl.pallas_call(
        _rmsnorm_kernel,
        grid=(s // tm,),
        in_specs=[pl.BlockSpec((tm, d), lambda i: (i, 0)),
                  pl.BlockSpec((1, d), lambda i: (0, 0))],
        out_specs=pl.BlockSpec((tm, d), lambda i: (i, 0)),
        out_shape=jax.ShapeDtypeStruct((s, d), BF16),
        compiler_params=_params("parallel"),
        name="rmsnorm_pre",
    )(x, gain.reshape(1, d))


class _AttnPlan(NamedTuple):
    nq: int
    units_per_head: int
    units: int


def _attn_unit(nq, hkv, t, nt, sink_ref, q_ref, kp_ref, km_ref, kn_ref, vp_ref, vm_ref, vn_ref,
               gate_ref, o_ref, s_scr, p_scr):
    rows = GROUP * BLOCK
    scale = HEAD_DIM ** -0.5
    kfull = jnp.concatenate([kp_ref[0], km_ref[0], kn_ref[0]], axis=0)
    vfull = jnp.concatenate([vp_ref[0], vm_ref[0], vn_ref[0]], axis=0)
    vext = jnp.concatenate([vfull, jnp.ones_like(vfull)], axis=1)

    for b in range(nq):
        q4 = q_ref[:, b * BLOCK:(b + 1) * BLOCK, :].reshape(rows, HEAD_DIM)
        s_scr[b] = lax.dot_general(q4, kfull[b * BLOCK:(b + 3) * BLOCK], (((1,), (1,)), ((), ())),
                                   preferred_element_type=F32)

    qi = lax.broadcasted_iota(jnp.int32, (rows, BLOCK), 0) % BLOCK
    kc = lax.broadcasted_iota(jnp.int32, (rows, BLOCK), 1)
    no_prev = jnp.where(t > 0, 0, BLOCK)
    no_next = jnp.where(t < nt - 1, 0, BLOCK)
    sink_col = jnp.concatenate(
        [jnp.full((BLOCK, 1), sink_ref[hkv * GROUP + g], F32) for g in range(GROUP)], axis=0)
    sink_terms = []
    for b in range(nq):
        ok_prev = kc >= (qi + no_prev if b == 0 else qi)
        ok_next = kc <= (qi - no_next if b == nq - 1 else qi)
        s0 = jnp.where(ok_prev, s_scr[b, :, 0:BLOCK], -jnp.inf)
        s1 = s_scr[b, :, BLOCK:2 * BLOCK]
        s2 = jnp.where(ok_next, s_scr[b, :, 2 * BLOCK:3 * BLOCK], -jnp.inf)
        m_raw = jnp.max(jnp.maximum(jnp.maximum(s0, s1), s2), axis=-1, keepdims=True)
        m = jnp.maximum(m_raw * scale, sink_col)
        m2 = m * LOG2E
        for k, sk in enumerate((s0, s1, s2)):
            p_scr[b, :, k * BLOCK:(k + 1) * BLOCK] = jnp.exp2(sk * (scale * LOG2E) - m2).astype(p_scr.dtype)
        sink_terms.append(jnp.exp(sink_col - m))

    for b in range(nq):
        o2 = jnp.dot(p_scr[b], vext[b * BLOCK:(b + 3) * BLOCK], preferred_element_type=F32)
        num = o2[:, :HEAD_DIM]
        den = o2[:, HEAD_DIM:] + sink_terms[b]
        for g in range(GROUP):
            gate = gate_ref[g, b * BLOCK:(b + 1) * BLOCK, :].astype(F32)
            r = slice(g * BLOCK, (g + 1) * BLOCK)
            o_ref[b * BLOCK:(b + 1) * BLOCK, g * HEAD_DIM:(g + 1) * HEAD_DIM] = (
                (num[r] * gate) / (den[r] * (1.0 + jnp.exp(-gate)))).astype(o_ref.dtype)


def _attn_specs(plan, n_i, n_q, n_kv, nb):
    nq, upk, units = plan
    tq = nq * BLOCK
    assert (n_q + 2 * n_kv) % GROUP == 0, "attn_gate heads must start on a GROUP boundary"
    k0, v0, g0 = n_q, n_q + n_kv, (n_q + 2 * n_kv) // GROUP

    def unit(j, i):
        u = jnp.minimum(j * n_i + i, units - 1)
        return u // upk, u % upk

    def at(fn):
        return lambda j, i: fn(*unit(j, i))

    def main(off):
        return pl.BlockSpec((1, tq, HEAD_DIM), at(lambda h, t: (off + h, t, 0)))

    def prev(off):
        return pl.BlockSpec((1, BLOCK, HEAD_DIM), at(lambda h, t: (off + h, jnp.maximum(t * nq - 1, 0), 0)))

    def nxt(off):
        return pl.BlockSpec((1, BLOCK, HEAD_DIM),
                            at(lambda h, t: (off + h, jnp.minimum((t + 1) * nq, nb - 1), 0)))

    in_specs = [pl.BlockSpec(memory_space=pltpu.SMEM),
                pl.BlockSpec((GROUP, tq, HEAD_DIM), at(lambda h, t: (h, t, 0))),
                prev(k0), main(k0), nxt(k0), prev(v0), main(v0), nxt(v0),
                pl.BlockSpec((GROUP, tq, HEAD_DIM), at(lambda h, t: (g0 + h, t, 0)))]
    out_spec = pl.BlockSpec((tq, GROUP * HEAD_DIM), at(lambda h, t: (t, h)))
    scratch = [pltpu.VMEM((nq, GROUP * BLOCK, 3 * BLOCK), F32),
               pltpu.VMEM((nq, GROUP * BLOCK, 3 * BLOCK), BF16)]
    return in_specs, out_spec, scratch


def _rotary_store(acc, tabs, o_ref):
    c, s1, s2 = (t[...] for t in tabs)
    for g in range(o_ref.shape[0]):
        t = acc[:, g * HEAD_DIM:(g + 1) * HEAD_DIM]
        r = (t * c + pltpu.roll(t, HEAD_DIM - ROT_HALF, 1) * s1
             + pltpu.roll(t, ROT_HALF, 1) * s2)
        o_ref[g] = r.astype(o_ref.dtype)


class _ProjCfg(NamedTuple):
    n_tabs: int
    n_riders: int
    kc: int
    col0: int
    attn: Optional[_AttnPlan]


N_ATTN_IN = 9


def _in_proj_kernel(cfg, h_ref, w_hbm, *refs):
    refs = list(refs)
    take = lambda n: [refs.pop(0) for _ in range(n)]
    tabs = take(cfg.n_tabs)
    riders_in = take(cfg.n_riders)
    attn_in = take(N_ATTN_IN if cfg.attn else 0)
    o_ref, = take(1)
    riders_out = take(cfg.n_riders)
    z_ref = take(1 if cfg.attn else 0)
    w_even, w_odd, stage, sem = take(4)
    attn_scratch = refs
    kc = cfg.kc
    j, i = pl.program_id(0), pl.program_id(1)
    n_j, n_i = pl.num_programs(0), pl.num_programs(1)
    tn = w_even.shape[1]
    step = j * n_i + i
    slot = step % 2

    def chunk_copy(jb, ic, to_slot):
        return pltpu.make_async_copy(
            w_hbm.at[pl.ds(ic * kc, kc), pl.ds(cfg.col0 + jb * tn, tn)],
            stage.at[to_slot], sem.at[to_slot])

    j_next = jnp.minimum(j + 1, n_j - 1)

    @pl.when(step == 0)
    def _():
        for ic in range(w_even.shape[0] // kc):
            cp = chunk_copy(0, ic, 0)
            cp.start()
            cp.wait()
            w_even[ic * kc:(ic + 1) * kc, :] = stage[0].astype(BF16)
        chunk_copy(j_next, 0, 0).start()

    @pl.when(step + 1 < n_j * n_i)
    def _():
        wrap = i + 1 == n_i
        chunk_copy(jnp.minimum(jnp.where(wrap, j + 2, j + 1), n_j - 1),
                   jnp.where(wrap, 0, i + 1), 1 - slot).start()

    chunk_copy(j_next, i, slot).wait()

    def compute(w_cur, w_next):
        w_next[pl.ds(pl.multiple_of(i * kc, kc), kc), :] = stage[slot].astype(BF16)
        for src, dst in zip(riders_in, riders_out):
            dst[...] = src[...].astype(dst.dtype)
        acc = jnp.dot(h_ref[...], w_cur[...], preferred_element_type=F32)
        if tabs:
            _rotary_store(acc, tabs, o_ref)
        else:
            o_ref[...] = acc.astype(o_ref.dtype)
        if cfg.attn:
            unit = jnp.minimum(step, cfg.attn.units - 1)
            _attn_unit(cfg.attn.nq, unit // cfg.attn.units_per_head, unit % cfg.attn.units_per_head,
                       cfg.attn.units_per_head, *attn_in, *z_ref, *attn_scratch)

    pl.when(j % 2 == 0)(lambda: compute(w_even, w_odd))
    pl.when(j % 2 == 1)(lambda: compute(w_odd, w_even))


def _plan_rest(n_cols, n_i, n_kv, nb):
    best = None
    for tn in range(1024, 0, -LANES):
        if n_cols % tn:
            continue
        steps = (n_cols // tn) * n_i
        for nq in (1, 2, 4, 8):
            if nb % nq or n_kv * (nb // nq) > steps:
                continue
            waste = steps - n_kv * (nb // nq)
            if best is None or waste < best[0]:
                best = (waste, tn, _AttnPlan(nq, nb // nq, n_kv * (nb // nq)))
            break
    assert best is not None, (n_cols, n_i, n_kv, nb)
    return best[1], best[2]


def _in_proj(h, w_in, col0, n_cols, tn, rot_cols, riders, attn, name):
    s, d = h.shape
    tm = _tile(s, 1024, 8)
    n_j, n_i = n_cols // tn, s // tm
    kc = d // n_i
    assert n_cols % tn == 0 and d % n_i == 0 and kc % BF16_SUBLANES == 0, (n_cols, tn, d, n_i)

    in_specs = [pl.BlockSpec((tm, d), lambda j, i: (i, 0)),
                pl.BlockSpec(memory_space=pl.ANY)]
    args = [h, w_in]
    if rot_cols:
        assert rot_cols % tn == 0 and tn % HEAD_DIM == 0
        n_rot_tiles = rot_cols // tn
        identity_block = n_i
        tab_spec = pl.BlockSpec(
            (tm, HEAD_DIM), lambda j, i: (jnp.where(j < n_rot_tiles, i, identity_block), 0))
        in_specs += [tab_spec] * 3
        args += list(_rotary_tables(s, tm))
        out_specs = [pl.BlockSpec((tn // HEAD_DIM, tm, HEAD_DIM), lambda j, i: (j, i, 0))]
        out_shape = [jax.ShapeDtypeStruct((n_cols // HEAD_DIM, s, HEAD_DIM), BF16)]
    else:
        out_specs = [pl.BlockSpec((tm, tn), lambda j, i: (i, j))]
        out_shape = [jax.ShapeDtypeStruct((s, n_cols), BF16)]

    for r in riders:
        rows, cols = r.shape
        rr = next(c for c in range(BF16_SUBLANES, rows + 1, BF16_SUBLANES)
                  if rows % c == 0 and rows // c <= n_j * n_i)
        spec = pl.BlockSpec((rr, cols), functools.partial(
            lambda j, i, last: (jnp.minimum(j * n_i + i, last), 0), last=rows // rr - 1))
        in_specs.append(spec)
        args.append(r)
        out_specs.append(spec)
        out_shape.append(jax.ShapeDtypeStruct(r.shape, BF16))

    scratch = [pltpu.VMEM((d, tn), BF16), pltpu.VMEM((d, tn), BF16),
               pltpu.VMEM((2, kc, tn), F32), pltpu.SemaphoreType.DMA((2,))]
    plan = None
    if attn is not None:
        plan, qkvg, sink, n_q, n_kv = attn
        a_in, a_out, a_scratch = _attn_specs(plan, n_i, n_q, n_kv, s // BLOCK)
        in_specs += a_in
        args += [sink] + [qkvg] * (N_ATTN_IN - 1)
        out_specs.append(a_out)
        out_shape.append(jax.ShapeDtypeStruct((s, n_q * HEAD_DIM), BF16))
        scratch += a_scratch

    cfg = _ProjCfg(3 if rot_cols else 0, len(riders), kc, col0, plan)
    return pl.pallas_call(
        functools.partial(_in_proj_kernel, cfg),
        grid=(n_j, n_i),
        in_specs=in_specs,
        out_specs=out_specs,
        out_shape=out_shape,
        scratch_shapes=scratch,
        compiler_params=_params("arbitrary", "arbitrary"),
        name=name,
    )(*args)


def _conv_kernel(bg_ref, cg_ref, cx_ref, gt_ref, cgp_ref, cxp_ref, cgn_ref, cxn_ref,
                 w_ref, b_ref, o_ref):
    i = pl.program_id(0)
    ni = pl.num_programs(0)
    tm = o_ref.shape[0]
    halo = cgp_ref.shape[0]
    u = cg_ref[...].astype(F32) * cx_ref[...].astype(F32)
    u_prev = cgp_ref[halo - 1:halo, :].astype(F32) * cxp_ref[halo - 1:halo, :].astype(F32)
    u_next = cgn_ref[0:1, :].astype(F32) * cxn_ref[0:1, :].astype(F32)
    u_prev = jnp.where(i > 0, u_prev, 0.0)
    u_next = jnp.where(i < ni - 1, u_next, 0.0)
    row = lax.broadcasted_iota(jnp.int32, u.shape, 0)
    up = jnp.where(row == 0, u_prev, pltpu.roll(u, 1, 0))
    dn = jnp.where(row == tm - 1, u_next, pltpu.roll(u, tm - 1, 0))
    c = up * w_ref[0:1, :] + u * w_ref[1:2, :] + dn * w_ref[2:3, :] + b_ref[...]
    gate = gt_ref[...].astype(F32)
    o_ref[...] = (bg_ref[...].astype(F32) * c * (gate * jax.nn.sigmoid(gate))).astype(o_ref.dtype)


def _short_conv(p_rest, conv_w, conv_b, d, col0):
    s = p_rest.shape[0]
    halo = BF16_SUBLANES
    tm = _tile(s, 512, halo)
    tc = _tile(math.gcd(d, col0), 1024, LANES)
    nc = d // tc
    hb = tm // halo
    last_hb = s // halo - 1

    def cur(k):
        return pl.BlockSpec((tm, tc), lambda i, c: (i, (col0 // tc) + k * nc + c))

    def prev(k):
        return pl.BlockSpec((halo, tc),
                            lambda i, c: (jnp.maximum(i * hb - 1, 0), (col0 // tc) + k * nc + c))

    def nxt(k):
        return pl.BlockSpec((halo, tc),
                            lambda i, c: (jnp.minimum((i + 1) * hb, last_hb), (col0 // tc) + k * nc + c))

    return pl.pallas_call(
        _conv_kernel,
        grid=(s // tm, nc),
        in_specs=[cur(0), cur(1), cur(2), cur(3), prev(1), prev(2), nxt(1), nxt(2),
                  pl.BlockSpec((3, tc), lambda i, c: (0, c)),
                  pl.BlockSpec((1, tc), lambda i, c: (0, c))],
        out_specs=pl.BlockSpec((tm, tc), lambda i, c: (i, c)),
        out_shape=jax.ShapeDtypeStruct((s, d), BF16),
        compiler_params=_params("parallel", "parallel"),
        name="short_conv",
    )(p_rest, p_rest, p_rest, p_rest, p_rest, p_rest, p_rest, p_rest, conv_w, conv_b.reshape(1, d))


def _merge_kernel(za_ref, zb_ref, wa_ref, wb_ref, la_ref, lb_ref, ba_ref, bb_ref, o_ref):
    ya = jnp.dot(za_ref[...], wa_ref[...], preferred_element_type=F32)
    yb = jnp.dot(zb_ref[...], wb_ref[...], preferred_element_type=F32)
    ga = jax.nn.sigmoid(la_ref[...].astype(F32) + ba_ref[...])
    gb = jax.nn.sigmoid(lb_ref[...].astype(F32) + bb_ref[...])
    o_ref[...] = (ga * ya + gb * yb).astype(o_ref.dtype)


def _out_merge(za, zb, wa, wb, p_rest, b_merge, col_a, col_b):
    s, d = za.shape
    tm = _tile(s, 1024, 8)
    tn = _tile(math.gcd(d, col_a), 512, LANES)
    nn = d // tn
    return pl.pallas_call(
        _merge_kernel,
        grid=(s // tm, nn),
        in_specs=[pl.BlockSpec((tm, d), lambda i, j: (i, 0)),
                  pl.BlockSpec((tm, d), lambda i, j: (i, 0)),
                  pl.BlockSpec((d, tn), lambda i, j: (0, j)),
                  pl.BlockSpec((d, tn), lambda i, j: (0, j)),
                  pl.BlockSpec((tm, tn), lambda i, j: (i, col_a // tn + j)),
                  pl.BlockSpec((tm, tn), lambda i, j: (i, col_b // tn + j)),
                  pl.BlockSpec((1, tn), lambda i, j: (0, j)),
                  pl.BlockSpec((1, tn), lambda i, j: (0, nn + j))],
        out_specs=pl.BlockSpec((tm, tn), lambda i, j: (i, j)),
        out_shape=jax.ShapeDtypeStruct((s, d), BF16),
        compiler_params=_params("parallel", "parallel"),
        name="out_proj_merge",
    )(za, zb, wa, wb, p_rest, p_rest, b_merge, b_merge)


def _final_kernel(m_ref, w_ref, x_ref, g_ref, o_ref):
    o = jnp.dot(m_ref[...], w_ref[...], preferred_element_type=F32)
    ms = jnp.mean(o * o, axis=-1, keepdims=True)
    o_ref[...] = x_ref[...] + o * lax.rsqrt(ms + RMS_EPS) * g_ref[...]


def _final(m, wo, x, gain):
    s, d = m.shape
    tm = _tile(s, 256, 8)
    return pl.pallas_call(
        _final_kernel,
        grid=(s // tm,),
        in_specs=[pl.BlockSpec((tm, d), lambda i: (i, 0)),
                  pl.BlockSpec((d, d), lambda i: (0, 0), pipeline_mode=pl.Buffered(1)),
                  pl.BlockSpec((tm, d), lambda i: (i, 0)),
                  pl.BlockSpec((1, d), lambda i: (0, 0))],
        out_specs=pl.BlockSpec((tm, d), lambda i: (i, 0)),
        out_shape=jax.ShapeDtypeStruct((s, d), F32),
        compiler_params=_params("parallel"),
        name="wo_norm_residual",
    )(m, wo, x, gain.reshape(1, d))


def _rotary_tables(s, extra):
    pos = jnp.arange(s, dtype=F32)
    inv_freq = ROPE_THETA ** (-jnp.arange(0, ROT_DIM, 2, dtype=F32) / ROT_DIM)
    ang = pos[:, None] * inv_freq[None, :]
    cos, sin = jnp.cos(ang), jnp.sin(ang)
    pad = jnp.zeros((s, HEAD_DIM - ROT_DIM), F32)
    zero = jnp.zeros((s, ROT_HALF), F32)
    ident = jnp.zeros((extra, HEAD_DIM), F32)
    c = jnp.concatenate([jnp.concatenate([cos, cos, pad + 1.0], axis=1), ident + 1.0], axis=0)
    s1 = jnp.concatenate([jnp.concatenate([-sin, zero, pad], axis=1), ident], axis=0)
    s2 = jnp.concatenate([jnp.concatenate([zero, sin, pad], axis=1), ident], axis=0)
    return c, s1, s2


def _layer(x, norm_pre, w_in, b_merge, sink, conv_w, conv_b, wa, wb, wo, norm_post):
    s, d = x.shape
    n_q = d // HEAD_DIM
    n_kv = max(n_q // GROUP, 1)
    attn_w, kv_w = n_q * HEAD_DIM, n_kv * HEAD_DIM
    rot_cols = attn_w + kv_w
    qkvg_cols = 2 * attn_w + 2 * kv_w
    rest_cols = w_in.shape[1] - qkvg_cols
    merge_a_col, merge_b_col = 4 * d, 5 * d

    h = _rmsnorm(x, norm_pre)
    tn_a = _tile(math.gcd(qkvg_cols, rot_cols), 1024, HEAD_DIM)
    qkvg, = _in_proj(h, w_in, 0, qkvg_cols, tn_a, rot_cols, [], None, "in_proj_qkvg")
    tn_b, plan = _plan_rest(rest_cols, s // _tile(s, 1024, 8), n_kv, s // BLOCK)
    p_rest, wa16, wb16, wo16, za = _in_proj(h, w_in, qkvg_cols, rest_cols, tn_b, 0, [wa, wb, wo],
                                            (plan, qkvg, sink, n_q, n_kv), "in_proj_rest_attn")
    zb = _short_conv(p_rest, conv_w, conv_b, d, 0)
    m = _out_merge(za, zb, wa16, wb16, p_rest, b_merge.reshape(1, 2 * d), merge_a_col, merge_b_col)
    return _final(m, wo16, x, norm_post)


@jax.jit
def kernel(x, norm_pre, w_in, b_merge, attn_sink, conv_w, conv_b, w_attn_out, w_conv_out, w_out, norm_post):
    b, s, d = x.shape
    depth = norm_pre.shape[0]
    outs = []
    for bi in range(b):
        xb = x.reshape(s, d) if b == 1 else x[bi]
        for l in range(depth):
            xb = _layer(xb, norm_pre[l], w_in[l], b_merge[l], attn_sink[l], conv_w[l], conv_b[l],
                        w_attn_out[l], w_conv_out[l], w_out[l], norm_post[l])
        outs.append(xb)
    return outs[0].reshape(1, s, d) if b == 1 else jnp.stack(outs, axis=0)
```

```python
import functools
import math
from typing import NamedTuple, Optional

import jax
import jax.numpy as jnp
from jax import lax
from jax.experimental import pallas as pl
from jax.experimental.pallas import tpu as pltpu

HEAD_DIM = 128
GROUP = 4
WINDOW = 128
BLOCK = 128
ROPE_THETA = 500000.0
ROT_DIM = HEAD_DIM // 4
ROT_HALF = ROT_DIM // 2
RMS_EPS = 1e-6
LOG2E = 1.4426950408889634
LANES = 128
BF16_SUBLANES = 16
V7X_VMEM_BYTES = 64 * 1024 * 1024
VMEM_LIMIT_BYTES = V7X_VMEM_BYTES - 6 * 1024 * 1024

F32 = jnp.float32
BF16 = jnp.bfloat16


def _tile(dim, pref, unit):
    t = min(pref, dim)
    t -= t % unit
    while t > unit and dim % t:
        t -= unit
    assert t >= unit and dim % t == 0, (dim, pref, unit)
    return t


def _params(*sem):
    return pltpu.CompilerParams(dimension_semantics=sem, vmem_limit_bytes=VMEM_LIMIT_BYTES)


def _rmsnorm_kernel(x_ref, g_ref, o_ref):
    x = x_ref[...]
    ms = jnp.mean(x * x, axis=-1, keepdims=True)
    o_ref[...] = (x * lax.rsqrt(ms + RMS_EPS) * g_ref[...]).astype(o_ref.dtype)


def _rmsnorm(x, gain):
    s, d = x.shape
    tm = _tile(s, 256, 8)
    return pl.pallas_call(
        _rmsnorm_kernel,
        grid=(s // tm,),
        in_specs=[pl.BlockSpec((tm, d), lambda i: (i, 0)),
                  pl.BlockSpec((1, d), lambda i: (0, 0))],
        out_specs=pl.BlockSpec((tm, d), lambda i: (i, 0)),
        out_shape=jax.ShapeDtypeStruct((s, d), BF16),
        compiler_params=_params("parallel"),
        name="rmsnorm_pre",
    )(x, gain.reshape(1, d))


class _AttnPlan(NamedTuple):
    nq: int
    units_per_head: int
    units: int


def _rotate(t, tab):
    c, s1, s2 = (tab[:, k * HEAD_DIM:(k + 1) * HEAD_DIM] for k in range(3))
    return t * c + pltpu.roll(t, HEAD_DIM - ROT_HALF, 1) * s1 + pltpu.roll(t, ROT_HALF, 1) * s2


def _attn_unit(nq, hkv, t, nt, sink_ref, q_ref, kvp_ref, kvm_ref, kvn_ref,
               gate_ref, tp_ref, tm_ref, tn_ref, o_ref, s_scr, p_scr):
    rows = GROUP * BLOCK
    scale = HEAD_DIM ** -0.5
    kfull = jnp.concatenate([kvp_ref[0, 0], kvm_ref[0, 0], kvn_ref[0, 0]], axis=0)
    tabs = jnp.concatenate([tp_ref[...], tm_ref[...], tn_ref[...]], axis=0)
    kfull = _rotate(kfull.astype(F32), tabs).astype(kfull.dtype)
    vfull = jnp.concatenate([kvp_ref[1, 0], kvm_ref[1, 0], kvn_ref[1, 0]], axis=0)
    vext = jnp.concatenate([vfull, jnp.ones_like(vfull)], axis=1)

    for b in range(nq):
        q4 = q_ref[:, b * BLOCK:(b + 1) * BLOCK, :].reshape(rows, HEAD_DIM)
        tab_q = jnp.concatenate([tm_ref[b * BLOCK:(b + 1) * BLOCK, :]] * GROUP, axis=0)
        q4 = _rotate(q4.astype(F32), tab_q).astype(q4.dtype)
        s_scr[b] = lax.dot_general(q4, kfull[b * BLOCK:(b + 3) * BLOCK], (((1,), (1,)), ((), ())),
                                   preferred_element_type=F32)

    qi = lax.broadcasted_iota(jnp.int32, (rows, BLOCK), 0) % BLOCK
    kc = lax.broadcasted_iota(jnp.int32, (rows, BLOCK), 1)
    no_prev = jnp.where(t > 0, 0, BLOCK)
    no_next = jnp.where(t < nt - 1, 0, BLOCK)
    sink_col = jnp.concatenate(
        [jnp.full((BLOCK, 1), sink_ref[hkv * GROUP + g], F32) for g in range(GROUP)], axis=0)
    sink_terms = []
    for b in range(nq):
        ok_prev = kc >= (qi + no_prev if b == 0 else qi)
        ok_next = kc <= (qi - no_next if b == nq - 1 else qi)
        s0 = jnp.where(ok_prev, s_scr[b, :, 0:BLOCK], -jnp.inf)
        s1 = s_scr[b, :, BLOCK:2 * BLOCK]
        s2 = jnp.where(ok_next, s_scr[b, :, 2 * BLOCK:3 * BLOCK], -jnp.inf)
        m_raw = jnp.max(jnp.maximum(jnp.maximum(s0, s1), s2), axis=-1, keepdims=True)
        m = jnp.maximum(m_raw * scale, sink_col)
        m2 = m * LOG2E
        for k, sk in enumerate((s0, s1, s2)):
            p_scr[b, :, k * BLOCK:(k + 1) * BLOCK] = jnp.exp2(sk * (scale * LOG2E) - m2).astype(p_scr.dtype)
        sink_terms.append(jnp.exp(sink_col - m))

    for b in range(nq):
        o2 = jnp.dot(p_scr[b], vext[b * BLOCK:(b + 3) * BLOCK], preferred_element_type=F32)
        num = o2[:, :HEAD_DIM]
        den = o2[:, HEAD_DIM:] + sink_terms[b]
        for g in range(GROUP):
            gate = gate_ref[g, b * BLOCK:(b + 1) * BLOCK, :].astype(F32)
            r = slice(g * BLOCK, (g + 1) * BLOCK)
            o_ref[b * BLOCK:(b + 1) * BLOCK, g * HEAD_DIM:(g + 1) * HEAD_DIM] = (
                (num[r] * gate) / (den[r] * (1.0 + jnp.exp(-gate)))).astype(o_ref.dtype)


def _attn_specs(plan, n_i, n_q, n_kv, nb):
    nq, upk, units = plan
    tq = nq * BLOCK
    assert (n_q + 2 * n_kv) % GROUP == 0, "attn_gate heads must start on a GROUP boundary"
    g0 = (n_q + 2 * n_kv) // GROUP
    kv_pair = GROUP // 2

    def unit(j, i):
        u = jnp.minimum(j * n_i + i, units - 1)
        return lax.div(u, upk), lax.rem(u, upk)

    def at(fn):
        return lambda j, i: fn(*unit(j, i))

    in_specs = [pl.BlockSpec(memory_space=pltpu.SMEM),
                pl.BlockSpec((GROUP, tq, HEAD_DIM), at(lambda h, t: (h, t, 0))),
                pl.BlockSpec((2, 1, BLOCK, HEAD_DIM),
                             at(lambda h, t: (kv_pair, h, jnp.maximum(t * nq - 1, 0), 0))),
                pl.BlockSpec((2, 1, tq, HEAD_DIM), at(lambda h, t: (kv_pair, h, t, 0))),
                pl.BlockSpec((2, 1, BLOCK, HEAD_DIM),
                             at(lambda h, t: (kv_pair, h, jnp.minimum((t + 1) * nq, nb - 1), 0))),
                pl.BlockSpec((GROUP, tq, HEAD_DIM), at(lambda h, t: (g0 + h, t, 0))),
                pl.BlockSpec((BLOCK, 3 * HEAD_DIM), at(lambda h, t: (jnp.maximum(t * nq - 1, 0), 0))),
                pl.BlockSpec((tq, 3 * HEAD_DIM), at(lambda h, t: (t, 0))),
                pl.BlockSpec((BLOCK, 3 * HEAD_DIM), at(lambda h, t: (jnp.minimum((t + 1) * nq, nb - 1), 0)))]
    out_spec = pl.BlockSpec((tq, GROUP * HEAD_DIM), at(lambda h, t: (t, h)))
    scratch = [pltpu.VMEM((nq, GROUP * BLOCK, 3 * BLOCK), F32),
               pltpu.VMEM((nq, GROUP * BLOCK, 3 * BLOCK), BF16)]
    return in_specs, out_spec, scratch


class _ProjCfg(NamedTuple):
    head_major: bool
    n_riders: int
    kc: int
    col0: int
    attn: Optional[_AttnPlan]


N_ATTN_IN = 9


def _in_proj_kernel(cfg, h_ref, w_hbm, *refs):
    refs = list(refs)
    take = lambda n: [refs.pop(0) for _ in range(n)]
    riders_in = take(cfg.n_riders)
    attn_in = take(N_ATTN_IN if cfg.attn else 0)
    o_ref, = take(1)
    riders_out = take(cfg.n_riders)
    z_ref = take(1 if cfg.attn else 0)
    w_even, w_odd, stage, sem = take(4)
    attn_scratch = refs
    kc = cfg.kc
    j, i = pl.program_id(0), pl.program_id(1)
    n_j, n_i = pl.num_programs(0), pl.num_programs(1)
    tn = w_even.shape[1]
    step = j * n_i + i
    slot = step % 2

    def chunk_copy(jb, ic, to_slot):
        return pltpu.make_async_copy(
            w_hbm.at[pl.ds(ic * kc, kc), pl.ds(cfg.col0 + jb * tn, tn)],
            stage.at[to_slot], sem.at[to_slot])

    j_next = jnp.minimum(j + 1, n_j - 1)

    @pl.when(step == 0)
    def _():
        for ic in range(w_even.shape[0] // kc):
            cp = chunk_copy(0, ic, 0)
            cp.start()
            cp.wait()
            w_even[ic * kc:(ic + 1) * kc, :] = stage[0].astype(BF16)
        chunk_copy(j_next, 0, 0).start()

    @pl.when(step + 1 < n_j * n_i)
    def _():
        wrap = i + 1 == n_i
        chunk_copy(jnp.minimum(jnp.where(wrap, j + 2, j + 1), n_j - 1),
                   jnp.where(wrap, 0, i + 1), 1 - slot).start()

    chunk_copy(j_next, i, slot).wait()

    def compute(w_cur, w_next):
        w_next[pl.ds(pl.multiple_of(i * kc, kc), kc), :] = stage[slot].astype(BF16)
        for src, dst in zip(riders_in, riders_out):
            dst[...] = src[...].astype(dst.dtype)
        acc = jnp.dot(h_ref[...], w_cur[...], preferred_element_type=F32)
        if cfg.head_major:
            for g in range(o_ref.shape[0]):
                o_ref[g] = acc[:, g * HEAD_DIM:(g + 1) * HEAD_DIM].astype(o_ref.dtype)
        else:
            o_ref[...] = acc.astype(o_ref.dtype)
        if cfg.attn:
            unit = jnp.minimum(step, cfg.attn.units - 1)
            _attn_unit(cfg.attn.nq, unit // cfg.attn.units_per_head, unit % cfg.attn.units_per_head,
                       cfg.attn.units_per_head, *attn_in, *z_ref, *attn_scratch)

    pl.when(j % 2 == 0)(lambda: compute(w_even, w_odd))
    pl.when(j % 2 == 1)(lambda: compute(w_odd, w_even))


def _plan_rest(n_cols, n_i, n_kv, nb):
    best = None
    for tn in range(1024, 0, -LANES):
        if n_cols % tn:
            continue
        steps = (n_cols // tn) * n_i
        for nq in (1, 2, 4, 8):
            if nb % nq or n_kv * (nb // nq) > steps:
                continue
            waste = steps - n_kv * (nb // nq)
            if best is None or waste < best[0]:
                best = (waste, tn, _AttnPlan(nq, nb // nq, n_kv * (nb // nq)))
            break
    assert best is not None, (n_cols, n_i, n_kv, nb)
    return best[1], best[2]


def _in_proj(h, w_in, col0, n_cols, tn, head_major, riders, attn, name):
    s, d = h.shape
    tm = _tile(s, 1024, 8)
    n_j, n_i = n_cols // tn, s // tm
    kc = d // n_i
    assert n_cols % tn == 0 and d % n_i == 0 and kc % BF16_SUBLANES == 0, (n_cols, tn, d, n_i)

    in_specs = [pl.BlockSpec((tm, d), lambda j, i: (i, 0)),
                pl.BlockSpec(memory_space=pl.ANY)]
    args = [h, w_in]
    if head_major:
        assert tn % HEAD_DIM == 0
        out_specs = [pl.BlockSpec((tn // HEAD_DIM, tm, HEAD_DIM), lambda j, i: (j, i, 0))]
        out_shape = [jax.ShapeDtypeStruct((n_cols // HEAD_DIM, s, HEAD_DIM), BF16)]
    else:
        out_specs = [pl.BlockSpec((tm, tn), lambda j, i: (i, j))]
        out_shape = [jax.ShapeDtypeStruct((s, n_cols), BF16)]

    for r in riders:
        rows, cols = r.shape
        rr = next(c for c in range(BF16_SUBLANES, rows + 1, BF16_SUBLANES)
                  if rows % c == 0 and rows // c <= n_j * n_i)
        spec = pl.BlockSpec((rr, cols), functools.partial(
            lambda j, i, last: (jnp.minimum(j * n_i + i, last), 0), last=rows // rr - 1))
        in_specs.append(spec)
        args.append(r)
        out_specs.append(spec)
        out_shape.append(jax.ShapeDtypeStruct(r.shape, BF16))

    scratch = [pltpu.VMEM((d, tn), BF16), pltpu.VMEM((d, tn), BF16),
               pltpu.VMEM((2, kc, tn), F32), pltpu.SemaphoreType.DMA((2,))]
    plan = None
    if attn is not None:
        plan, qkvg, tabs, sink, n_q, n_kv = attn
        a_in, a_out, a_scratch = _attn_specs(plan, n_i, n_q, n_kv, s // BLOCK)
        in_specs += a_in
        kv4 = qkvg.reshape(-1, n_kv, s, HEAD_DIM)
        args += [sink, qkvg, kv4, kv4, kv4, qkvg] + [tabs] * 3
        out_specs.append(a_out)
        out_shape.append(jax.ShapeDtypeStruct((s, n_q * HEAD_DIM), BF16))
        scratch += a_scratch

    cfg = _ProjCfg(head_major, len(riders), kc, col0, plan)
    return pl.pallas_call(
        functools.partial(_in_proj_kernel, cfg),
        grid=(n_j, n_i),
        in_specs=in_specs,
        out_specs=out_specs,
        out_shape=out_shape,
        scratch_shapes=scratch,
        compiler_params=_params("arbitrary", "arbitrary"),
        name=name,
    )(*args)


def _conv_block(i, ni, bg_ref, cg_ref, cx_ref, gt_ref, cgp_ref, cxp_ref, cgn_ref, cxn_ref, w_ref, b_ref):
    tm = bg_ref.shape[0]
    halo = cgp_ref.shape[0]
    u = cg_ref[...].astype(F32) * cx_ref[...].astype(F32)
    u_prev = cgp_ref[halo - 1:halo, :].astype(F32) * cxp_ref[halo - 1:halo, :].astype(F32)
    u_next = cgn_ref[0:1, :].astype(F32) * cxn_ref[0:1, :].astype(F32)
    u_prev = jnp.where(i > 0, u_prev, 0.0)
    u_next = jnp.where(i < ni - 1, u_next, 0.0)
    row = lax.broadcasted_iota(jnp.int32, u.shape, 0)
    up = jnp.where(row == 0, u_prev, pltpu.roll(u, 1, 0))
    dn = jnp.where(row == tm - 1, u_next, pltpu.roll(u, tm - 1, 0))
    c = up * w_ref[0:1, :] + u * w_ref[1:2, :] + dn * w_ref[2:3, :] + b_ref[...]
    gate = gt_ref[...].astype(F32)
    return bg_ref[...].astype(F32) * c * (gate * jax.nn.sigmoid(gate))


def _conv_kernel(*refs):
    o_ref = refs[-1]
    o_ref[...] = _conv_block(pl.program_id(0), pl.num_programs(0), *refs[:-1]).astype(o_ref.dtype)


def _short_conv(p_rest, conv_w, conv_b, d, col0):
    s = p_rest.shape[0]
    halo = BF16_SUBLANES
    tm = _tile(s, 512, halo)
    tc = _tile(math.gcd(d, col0), 1024, LANES)
    nc = d // tc
    hb = tm // halo
    last_hb = s // halo - 1

    def cur(k):
        return pl.BlockSpec((tm, tc), lambda i, c: (i, (col0 // tc) + k * nc + c))

    def prev(k):
        return pl.BlockSpec((halo, tc),
                            lambda i, c: (jnp.maximum(i * hb - 1, 0), (col0 // tc) + k * nc + c))

    def nxt(k):
        return pl.BlockSpec((halo, tc),
                            lambda i, c: (jnp.minimum((i + 1) * hb, last_hb), (col0 // tc) + k * nc + c))

    return pl.pallas_call(
        _conv_kernel,
        grid=(s // tm, nc),
        in_specs=[cur(0), cur(1), cur(2), cur(3), prev(1), prev(2), nxt(1), nxt(2),
                  pl.BlockSpec((3, tc), lambda i, c: (0, c)),
                  pl.BlockSpec((1, tc), lambda i, c: (0, c))],
        out_specs=pl.BlockSpec((tm, tc), lambda i, c: (i, c)),
        out_shape=jax.ShapeDtypeStruct((s, d), BF16),
        compiler_params=_params("parallel", "parallel"),
        name="short_conv",
    )(p_rest, p_rest, p_rest, p_rest, p_rest, p_rest, p_rest, p_rest, conv_w, conv_b.reshape(1, d))


def _merge_kernel(za_ref, zb_ref, wa_ref, wb_ref, la_ref, lb_ref, ba_ref, bb_ref, o_ref):
    ya = jnp.dot(za_ref[...], wa_ref[...], preferred_element_type=F32)
    yb = jnp.dot(zb_ref[...], wb_ref[...], preferred_element_type=F32)
    ga = jax.nn.sigmoid(la_ref[...].astype(F32) + ba_ref[...])
    gb = jax.nn.sigmoid(lb_ref[...].astype(F32) + bb_ref[...])
    o_ref[...] = (ga * ya + gb * yb).astype(o_ref.dtype)


def _out_merge(za, zb, wa, wb, p_rest, b_merge, col_a, col_b):
    s, d = za.shape
    tm = _tile(s, 1024, 8)
    tn = _tile(math.gcd(d, col_a), 512, LANES)
    nn = d // tn
    return pl.pallas_call(
        _merge_kernel,
        grid=(s // tm, nn),
        in_specs=[pl.BlockSpec((tm, d), lambda i, j: (i, 0)),
                  pl.BlockSpec((tm, d), lambda i, j: (i, 0)),
                  pl.BlockSpec((d, tn), lambda i, j: (0, j)),
                  pl.BlockSpec((d, tn), lambda i, j: (0, j)),
                  pl.BlockSpec((tm, tn), lambda i, j: (i, col_a // tn + j)),
                  pl.BlockSpec((tm, tn), lambda i, j: (i, col_b // tn + j)),
                  pl.BlockSpec((1, tn), lambda i, j: (0, j)),
                  pl.BlockSpec((1, tn), lambda i, j: (0, nn + j))],
        out_specs=pl.BlockSpec((tm, tn), lambda i, j: (i, j)),
        out_shape=jax.ShapeDtypeStruct((s, d), BF16),
        compiler_params=_params("parallel", "parallel"),
        name="out_proj_merge",
    )(za, zb, wa, wb, p_rest, p_rest, b_merge, b_merge)


def _final_kernel(m_ref, w_ref, x_ref, g_ref, o_ref):
    o = jnp.dot(m_ref[...], w_ref[...], preferred_element_type=F32)
    ms = jnp.mean(o * o, axis=-1, keepdims=True)
    o_ref[...] = x_ref[...] + o * lax.rsqrt(ms + RMS_EPS) * g_ref[...]


def _final(m, wo, x, gain):
    s, d = m.shape
    tm = _tile(s, 256, 8)
    return pl.pallas_call(
        _final_kernel,
        grid=(s // tm,),
        in_specs=[pl.BlockSpec((tm, d), lambda i: (i, 0)),
                  pl.BlockSpec((d, d), lambda i: (0, 0), pipeline_mode=pl.Buffered(1)),
                  pl.BlockSpec((tm, d), lambda i: (i, 0)),
                  pl.BlockSpec((1, d), lambda i: (0, 0))],
        out_specs=pl.BlockSpec((tm, d), lambda i: (i, 0)),
        out_shape=jax.ShapeDtypeStruct((s, d), F32),
        compiler_params=_params("parallel"),
        name="wo_norm_residual",
    )(m, wo, x, gain.reshape(1, d))


def _rotary_tables(s):
    pos = jnp.arange(s, dtype=F32)
    inv_freq = ROPE_THETA ** (-jnp.arange(0, ROT_DIM, 2, dtype=F32) / ROT_DIM)
    ang = pos[:, None] * inv_freq[None, :]
    cos, sin = jnp.cos(ang), jnp.sin(ang)
    pad = jnp.zeros((s, HEAD_DIM - ROT_DIM), F32)
    zero = jnp.zeros((s, ROT_HALF), F32)
    return jnp.concatenate([cos, cos, pad + 1.0, -sin, zero, pad, zero, sin, pad], axis=1)


def _layer(x, norm_pre, w_in, b_merge, sink, conv_w, conv_b, wa, wb, wo, norm_post):
    s, d = x.shape
    n_q = d // HEAD_DIM
    n_kv = max(n_q // GROUP, 1)
    attn_w, kv_w = n_q * HEAD_DIM, n_kv * HEAD_DIM
    qkvg_cols = 2 * attn_w + 2 * kv_w
    rest_cols = w_in.shape[1] - qkvg_cols
    merge_a_col, merge_b_col = 4 * d, 5 * d

    h = _rmsnorm(x, norm_pre)
    assert kv_w <= 1024 and qkvg_cols % kv_w == 0, (kv_w, qkvg_cols)
    qkvg, wa16, wb16, wo16 = _in_proj(h, w_in, 0, qkvg_cols, kv_w, True, [wa, wb, wo], None,
                                      "in_proj_qkvg")
    tn_b, plan = _plan_rest(rest_cols, s // _tile(s, 1024, 8), n_kv, s // BLOCK)
    p_rest, za = _in_proj(h, w_in, qkvg_cols, rest_cols, tn_b, False, [],
                          (plan, qkvg, _rotary_tables(s), sink, n_q, n_kv), "in_proj_rest_attn")
    zb = _short_conv(p_rest, conv_w, conv_b, d, 0)
    m = _out_merge(za, zb, wa16, wb16, p_rest, b_merge.reshape(1, 2 * d), merge_a_col, merge_b_col)
    return _final(m, wo16, x, norm_post)


@jax.jit
def kernel(x, norm_pre, w_in, b_merge, attn_sink, conv_w, conv_b, w_attn_out, w_conv_out, w_out, norm_post):
    b, s, d = x.shape
    depth = norm_pre.shape[0]
    outs = []
    for bi in range(b):
        xb = x.reshape(s, d) if b == 1 else x[bi]
        for l in range(depth):
            xb = _layer(xb, norm_pre[l], w_in[l], b_merge[l], attn_sink[l], conv_w[l], conv_b[l],
                        w_attn_out[l], w_conv_out[l], w_out[l], norm_post[l])
        outs.append(xb)
    return outs[0].reshape(1, s, d) if b == 1 else jnp.stack(outs, axis=0)
```

```python
import functools
import math
from typing import NamedTuple, Optional

import jax
import jax.numpy as jnp
from jax import lax
from jax.experimental import pallas as pl
from jax.experimental.pallas import tpu as pltpu

HEAD_DIM = 128
GROUP = 4
WINDOW = 128
BLOCK = 128
ROPE_THETA = 500000.0
ROT_DIM = HEAD_DIM // 4
ROT_HALF = ROT_DIM // 2
RMS_EPS = 1e-6
LOG2E = 1.4426950408889634
LANES = 128
BF16_SUBLANES = 16
V7X_VMEM_BYTES = 64 * 1024 * 1024
VMEM_LIMIT_BYTES = V7X_VMEM_BYTES - 6 * 1024 * 1024

F32 = jnp.float32
BF16 = jnp.bfloat16


def _tile(dim, pref, unit):
    t = min(pref, dim)
    t -= t % unit
    while t > unit and dim % t:
        t -= unit
    assert t >= unit and dim % t == 0, (dim, pref, unit)
    return t


def _params(*sem):
    return pltpu.CompilerParams(dimension_semantics=sem, vmem_limit_bytes=VMEM_LIMIT_BYTES)


def _rmsnorm_kernel(x_ref, g_ref, o_ref):
    x = x_ref[...]
    ms = jnp.mean(x * x, axis=-1, keepdims=True)
    o_ref[...] = (x * lax.rsqrt(ms + RMS_EPS) * g_ref[...]).astype(o_ref.dtype)


def _rmsnorm(x, gain):
    s, d = x.shape
    tm = _tile(s, 512, 8)
    return pl.pallas_call(
        _rmsnorm_kernel,
        grid=(s // tm,),
        in_specs=[pl.BlockSpec((tm, d), lambda i: (i, 0)),
                  pl.BlockSpec((1, d), lambda i: (0, 0))],
        out_specs=pl.BlockSpec((tm, d), lambda i: (i, 0)),
        out_shape=jax.ShapeDtypeStruct((s, d), BF16),
        compiler_params=_params("parallel"),
        name="rmsnorm_pre",
    )(x, gain.reshape(1, d))


class _AttnPlan(NamedTuple):
    nq: int
    units_per_head: int
    units: int


def _rotary_lanes(p):
    lane = lax.broadcasted_iota(jnp.int32, p.shape, 1)
    first, second = lane < ROT_HALF, (lane >= ROT_HALF) & (lane < ROT_DIM)
    c = jnp.where(first, p, jnp.where(second, pltpu.roll(p, ROT_HALF, 1), 1.0))
    s1 = jnp.where(first, -pltpu.roll(p, HEAD_DIM - ROT_HALF, 1), 0.0)
    s2 = jnp.where(second, p, 0.0)
    return c, s1, s2


def _rotate(t, tabs):
    c, s1, s2 = tabs
    return t * c + pltpu.roll(t, HEAD_DIM - ROT_HALF, 1) * s1 + pltpu.roll(t, ROT_HALF, 1) * s2


def _attn_unit(nq, hkv, t, nt, sink_ref, q_ref, kvp_ref, kvm_ref, kvn_ref,
               gate_ref, tp_ref, tm_ref, tn_ref, o_ref, s_scr, p_scr):
    rows = GROUP * BLOCK
    scale = HEAD_DIM ** -0.5
    kfull = jnp.concatenate([kvp_ref[0, 0], kvm_ref[0, 0], kvn_ref[0, 0]], axis=0)
    tab_m = _rotary_lanes(tm_ref[...])
    tabs = [jnp.concatenate(parts, axis=0)
            for parts in zip(_rotary_lanes(tp_ref[...]), tab_m, _rotary_lanes(tn_ref[...]))]
    kfull = _rotate(kfull.astype(F32), tabs).astype(kfull.dtype)
    vfull = jnp.concatenate([kvp_ref[1, 0], kvm_ref[1, 0], kvn_ref[1, 0]], axis=0)
    vext = jnp.concatenate([vfull, jnp.ones_like(vfull)], axis=1)

    for b in range(nq):
        q4 = q_ref[:, b * BLOCK:(b + 1) * BLOCK, :].reshape(rows, HEAD_DIM)
        tab_q = [jnp.concatenate([x[b * BLOCK:(b + 1) * BLOCK]] * GROUP, axis=0) for x in tab_m]
        q4 = _rotate(q4.astype(F32), tab_q).astype(q4.dtype)
        s_scr[b] = lax.dot_general(q4, kfull[b * BLOCK:(b + 3) * BLOCK], (((1,), (1,)), ((), ())),
                                   preferred_element_type=F32)

    qi = lax.broadcasted_iota(jnp.int32, (rows, BLOCK), 0) % BLOCK
    kc = lax.broadcasted_iota(jnp.int32, (rows, BLOCK), 1)
    no_prev = jnp.where(t > 0, 0, BLOCK)
    no_next = jnp.where(t < nt - 1, 0, BLOCK)
    sink_col = jnp.concatenate(
        [jnp.full((BLOCK, 1), sink_ref[hkv * GROUP + g], F32) for g in range(GROUP)], axis=0)
    sink_terms = []
    for b in range(nq):
        ok_prev = kc >= (qi + no_prev if b == 0 else qi)
        ok_next = kc <= (qi - no_next if b == nq - 1 else qi)
        s0 = jnp.where(ok_prev, s_scr[b, :, 0:BLOCK], -jnp.inf)
        s1 = s_scr[b, :, BLOCK:2 * BLOCK]
        s2 = jnp.where(ok_next, s_scr[b, :, 2 * BLOCK:3 * BLOCK], -jnp.inf)
        m_raw = jnp.max(jnp.maximum(jnp.maximum(s0, s1), s2), axis=-1, keepdims=True)
        m = jnp.maximum(m_raw * scale, sink_col)
        m2 = m * LOG2E
        for k, sk in enumerate((s0, s1, s2)):
            p_scr[b, :, k * BLOCK:(k + 1) * BLOCK] = jnp.exp2(sk * (scale * LOG2E) - m2).astype(p_scr.dtype)
        sink_terms.append(jnp.exp(sink_col - m))

    for b in range(nq):
        o2 = jnp.dot(p_scr[b], vext[b * BLOCK:(b + 3) * BLOCK], preferred_element_type=F32)
        num = o2[:, :HEAD_DIM]
        den = o2[:, HEAD_DIM:] + sink_terms[b]
        for g in range(GROUP):
            gate = gate_ref[g, b * BLOCK:(b + 1) * BLOCK, :].astype(F32)
            r = slice(g * BLOCK, (g + 1) * BLOCK)
            o_ref[b * BLOCK:(b + 1) * BLOCK, g * HEAD_DIM:(g + 1) * HEAD_DIM] = (
                (num[r] * gate) / (den[r] * (1.0 + jnp.exp(-gate)))).astype(o_ref.dtype)


def _attn_specs(plan, n_i, n_q, n_kv, nb):
    nq, upk, units = plan
    tq = nq * BLOCK
    assert (n_q + 2 * n_kv) % GROUP == 0, "attn_gate heads must start on a GROUP boundary"
    g0 = (n_q + 2 * n_kv) // GROUP
    kv_pair = GROUP // 2

    def unit(j, i):
        u = jnp.minimum(j * n_i + i, units - 1)
        return lax.div(u, upk), lax.rem(u, upk)

    def at(fn):
        return lambda j, i: fn(*unit(j, i))

    in_specs = [pl.BlockSpec(memory_space=pltpu.SMEM),
                pl.BlockSpec((GROUP, tq, HEAD_DIM), at(lambda h, t: (h, t, 0))),
                pl.BlockSpec((2, 1, BLOCK, HEAD_DIM),
                             at(lambda h, t: (kv_pair, h, jnp.maximum(t * nq - 1, 0), 0))),
                pl.BlockSpec((2, 1, tq, HEAD_DIM), at(lambda h, t: (kv_pair, h, t, 0))),
                pl.BlockSpec((2, 1, BLOCK, HEAD_DIM),
                             at(lambda h, t: (kv_pair, h, jnp.minimum((t + 1) * nq, nb - 1), 0))),
                pl.BlockSpec((GROUP, tq, HEAD_DIM), at(lambda h, t: (g0 + h, t, 0))),
                pl.BlockSpec((BLOCK, HEAD_DIM), at(lambda h, t: (jnp.maximum(t * nq - 1, 0), 0))),
                pl.BlockSpec((tq, HEAD_DIM), at(lambda h, t: (t, 0))),
                pl.BlockSpec((BLOCK, HEAD_DIM), at(lambda h, t: (jnp.minimum((t + 1) * nq, nb - 1), 0)))]
    out_spec = pl.BlockSpec((tq, GROUP * HEAD_DIM), at(lambda h, t: (t, h)))
    scratch = [pltpu.VMEM((nq, GROUP * BLOCK, 3 * BLOCK), F32),
               pltpu.VMEM((nq, GROUP * BLOCK, 3 * BLOCK), BF16)]
    return in_specs, out_spec, scratch


class _ProjCfg(NamedTuple):
    head_major: bool
    n_riders: int
    kc: int
    col0: int
    attn: Optional[_AttnPlan]


N_ATTN_IN = 9


def _in_proj_kernel(cfg, h_ref, w_hbm, *refs):
    refs = list(refs)
    take = lambda n: [refs.pop(0) for _ in range(n)]
    riders_in = take(cfg.n_riders)
    attn_in = take(N_ATTN_IN if cfg.attn else 0)
    o_ref, = take(1)
    riders_out = take(cfg.n_riders)
    z_ref = take(1 if cfg.attn else 0)
    w_even, w_odd, stage, sem = take(4)
    attn_scratch = refs
    kc = cfg.kc
    j, i = pl.program_id(0), pl.program_id(1)
    n_j, n_i = pl.num_programs(0), pl.num_programs(1)
    tn = w_even.shape[1]
    step = j * n_i + i
    slot = step % 2

    def chunk_copy(jb, ic, to_slot):
        return pltpu.make_async_copy(
            w_hbm.at[pl.ds(ic * kc, kc), pl.ds(cfg.col0 + jb * tn, tn)],
            stage.at[to_slot], sem.at[to_slot])

    j_next = jnp.minimum(j + 1, n_j - 1)

    @pl.when(step == 0)
    def _():
        for ic in range(w_even.shape[0] // kc):
            cp = chunk_copy(0, ic, 0)
            cp.start()
            cp.wait()
            w_even[ic * kc:(ic + 1) * kc, :] = stage[0].astype(BF16)
        chunk_copy(j_next, 0, 0).start()

    @pl.when(step + 1 < n_j * n_i)
    def _():
        wrap = i + 1 == n_i
        chunk_copy(jnp.minimum(jnp.where(wrap, j + 2, j + 1), n_j - 1),
                   jnp.where(wrap, 0, i + 1), 1 - slot).start()

    chunk_copy(j_next, i, slot).wait()

    def compute(w_cur, w_next):
        for src, dst in zip(riders_in, riders_out):
            dst[...] = src[...].astype(dst.dtype)
        acc = jnp.dot(h_ref[...], w_cur[...], preferred_element_type=F32)
        if cfg.head_major:
            for g in range(o_ref.shape[0]):
                o_ref[g] = acc[:, g * HEAD_DIM:(g + 1) * HEAD_DIM].astype(o_ref.dtype)
        else:
            o_ref[...] = acc.astype(o_ref.dtype)
        w_next[pl.ds(pl.multiple_of(i * kc, kc), kc), :] = stage[slot].astype(BF16)
        if cfg.attn:
            unit = jnp.minimum(step, cfg.attn.units - 1)
            _attn_unit(cfg.attn.nq, unit // cfg.attn.units_per_head, unit % cfg.attn.units_per_head,
                       cfg.attn.units_per_head, *attn_in, *z_ref, *attn_scratch)

    pl.when(j % 2 == 0)(lambda: compute(w_even, w_odd))
    pl.when(j % 2 == 1)(lambda: compute(w_odd, w_even))


def _plan_rest(n_cols, n_i, n_kv, nb):
    best = None
    for tn in range(1024, 0, -LANES):
        if n_cols % tn:
            continue
        steps = (n_cols // tn) * n_i
        for nq in (1, 2, 4, 8):
            if nb % nq or n_kv * (nb // nq) > steps:
                continue
            waste = steps - n_kv * (nb // nq)
            if best is None or waste < best[0]:
                best = (waste, tn, _AttnPlan(nq, nb // nq, n_kv * (nb // nq)))
            break
    assert best is not None, (n_cols, n_i, n_kv, nb)
    return best[1], best[2]


def _in_proj(h, w_in, col0, n_cols, tn, head_major, riders, attn, name):
    s, d = h.shape
    tm = _tile(s, 1024, 8)
    n_j, n_i = n_cols // tn, s // tm
    kc = d // n_i
    assert n_cols % tn == 0 and d % n_i == 0 and kc % BF16_SUBLANES == 0, (n_cols, tn, d, n_i)

    in_specs = [pl.BlockSpec((tm, d), lambda j, i: (i, 0)),
                pl.BlockSpec(memory_space=pl.ANY)]
    args = [h, w_in]
    if head_major:
        assert tn % HEAD_DIM == 0
        out_specs = [pl.BlockSpec((tn // HEAD_DIM, tm, HEAD_DIM), lambda j, i: (j, i, 0))]
        out_shape = [jax.ShapeDtypeStruct((n_cols // HEAD_DIM, s, HEAD_DIM), BF16)]
    else:
        out_specs = [pl.BlockSpec((tm, tn), lambda j, i: (i, j))]
        out_shape = [jax.ShapeDtypeStruct((s, n_cols), BF16)]

    for r in riders:
        rows, cols = r.shape
        rr = next(c for c in range(BF16_SUBLANES, rows + 1, BF16_SUBLANES)
                  if rows % c == 0 and rows // c <= n_j * n_i)
        spec = pl.BlockSpec((rr, cols), functools.partial(
            lambda j, i, last: (jnp.minimum(j * n_i + i, last), 0), last=rows // rr - 1))
        in_specs.append(spec)
        args.append(r)
        out_specs.append(spec)
        out_shape.append(jax.ShapeDtypeStruct(r.shape, BF16))

    scratch = [pltpu.VMEM((d, tn), BF16), pltpu.VMEM((d, tn), BF16),
               pltpu.VMEM((2, kc, tn), F32), pltpu.SemaphoreType.DMA((2,))]
    plan = None
    if attn is not None:
        plan, qkvg, tabs, sink, n_q, n_kv = attn
        a_in, a_out, a_scratch = _attn_specs(plan, n_i, n_q, n_kv, s // BLOCK)
        in_specs += a_in
        kv4 = qkvg.reshape(-1, n_kv, s, HEAD_DIM)
        args += [sink, qkvg, kv4, kv4, kv4, qkvg] + [tabs] * 3
        out_specs.append(a_out)
        out_shape.append(jax.ShapeDtypeStruct((s, n_q * HEAD_DIM), BF16))
        scratch += a_scratch

    cfg = _ProjCfg(head_major, len(riders), kc, col0, plan)
    return pl.pallas_call(
        functools.partial(_in_proj_kernel, cfg),
        grid=(n_j, n_i),
        in_specs=in_specs,
        out_specs=out_specs,
        out_shape=out_shape,
        scratch_shapes=scratch,
        compiler_params=_params("arbitrary", "arbitrary"),
        name=name,
    )(*args)


def _conv_block(i, ni, bg_ref, cg_ref, cx_ref, gt_ref, cgp_ref, cxp_ref, cgn_ref, cxn_ref, w_ref, b_ref):
    tm = bg_ref.shape[0]
    halo = cgp_ref.shape[0]
    u = cg_ref[...].astype(F32) * cx_ref[...].astype(F32)
    u_prev = cgp_ref[halo - 1:halo, :].astype(F32) * cxp_ref[halo - 1:halo, :].astype(F32)
    u_next = cgn_ref[0:1, :].astype(F32) * cxn_ref[0:1, :].astype(F32)
    u_prev = jnp.where(i > 0, u_prev, 0.0)
    u_next = jnp.where(i < ni - 1, u_next, 0.0)
    row = lax.broadcasted_iota(jnp.int32, u.shape, 0)
    up = jnp.where(row == 0, u_prev, pltpu.roll(u, 1, 0))
    dn = jnp.where(row == tm - 1, u_next, pltpu.roll(u, tm - 1, 0))
    c = up * w_ref[0:1, :] + u * w_ref[1:2, :] + dn * w_ref[2:3, :] + b_ref[...]
    gate = gt_ref[...].astype(F32)
    return bg_ref[...].astype(F32) * c * (gate * jax.nn.sigmoid(gate))


def _conv_kernel(*refs):
    o_ref = refs[-1]
    o_ref[...] = _conv_block(pl.program_id(0), pl.num_programs(0), *refs[:-1]).astype(o_ref.dtype)


def _short_conv(p_rest, conv_w, conv_b, d, col0):
    s = p_rest.shape[0]
    halo = BF16_SUBLANES
    tm = _tile(s, 1024, halo)
    tc = _tile(math.gcd(d, col0), 1024, LANES)
    nc = d // tc
    hb = tm // halo
    last_hb = s // halo - 1

    def cur(k):
        return pl.BlockSpec((tm, tc), lambda i, c: (i, (col0 // tc) + k * nc + c))

    def prev(k):
        return pl.BlockSpec((halo, tc),
                            lambda i, c: (jnp.maximum(i * hb - 1, 0), (col0 // tc) + k * nc + c))

    def nxt(k):
        return pl.BlockSpec((halo, tc),
                            lambda i, c: (jnp.minimum((i + 1) * hb, last_hb), (col0 // tc) + k * nc + c))

    return pl.pallas_call(
        _conv_kernel,
        grid=(s // tm, nc),
        in_specs=[cur(0), cur(1), cur(2), cur(3), prev(1), prev(2), nxt(1), nxt(2),
                  pl.BlockSpec((3, tc), lambda i, c: (0, c)),
                  pl.BlockSpec((1, tc), lambda i, c: (0, c))],
        out_specs=pl.BlockSpec((tm, tc), lambda i, c: (i, c)),
        out_shape=jax.ShapeDtypeStruct((s, d), BF16),
        compiler_params=_params("parallel", "parallel"),
        name="short_conv",
    )(p_rest, p_rest, p_rest, p_rest, p_rest, p_rest, p_rest, p_rest, conv_w, conv_b.reshape(1, d))


def _merge_kernel(za_ref, zb_ref, wa_ref, wb_ref, la_ref, lb_ref, ba_ref, bb_ref, o_ref):
    ya = jnp.dot(za_ref[...], wa_ref[...], preferred_element_type=F32)
    yb = jnp.dot(zb_ref[...], wb_ref[...], preferred_element_type=F32)
    ga = jax.nn.sigmoid(la_ref[...].astype(F32) + ba_ref[...])
    gb = jax.nn.sigmoid(lb_ref[...].astype(F32) + bb_ref[...])
    o_ref[...] = (ga * ya + gb * yb).astype(o_ref.dtype)


def _out_merge(za, zb, wa, wb, p_rest, b_merge, col_a, col_b):
    s, d = za.shape
    tm = _tile(s, 1024, 8)
    tn = _tile(math.gcd(d, col_a), 512, LANES)
    nn = d // tn
    return pl.pallas_call(
        _merge_kernel,
        grid=(s // tm, nn),
        in_specs=[pl.BlockSpec((tm, d), lambda i, j: (i, 0)),
                  pl.BlockSpec((tm, d), lambda i, j: (i, 0)),
                  pl.BlockSpec((d, tn), lambda i, j: (0, j)),
                  pl.BlockSpec((d, tn), lambda i, j: (0, j)),
                  pl.BlockSpec((tm, tn), lambda i, j: (i, col_a // tn + j)),
                  pl.BlockSpec((tm, tn), lambda i, j: (i, col_b // tn + j)),
                  pl.BlockSpec((1, tn), lambda i, j: (0, j)),
                  pl.BlockSpec((1, tn), lambda i, j: (0, nn + j))],
        out_specs=pl.BlockSpec((tm, tn), lambda i, j: (i, j)),
        out_shape=jax.ShapeDtypeStruct((s, d), BF16),
        compiler_params=_params("parallel", "parallel"),
        name="out_proj_merge",
    )(za, zb, wa, wb, p_rest, p_rest, b_merge, b_merge)


def _final_kernel(m_ref, w_ref, x_ref, g_ref, o_ref):
    o = jnp.dot(m_ref[...], w_ref[...], preferred_element_type=F32)
    ms = jnp.mean(o * o, axis=-1, keepdims=True)
    o_ref[...] = x_ref[...] + o * lax.rsqrt(ms + RMS_EPS) * g_ref[...]


def _final(m, wo, x, gain):
    s, d = m.shape
    tm = _tile(s, 256, 8)
    return pl.pallas_call(
        _final_kernel,
        grid=(s // tm,),
        in_specs=[pl.BlockSpec((tm, d), lambda i: (i, 0)),
                  pl.BlockSpec((d, d), lambda i: (0, 0), pipeline_mode=pl.Buffered(1)),
                  pl.BlockSpec((tm, d), lambda i: (i, 0)),
                  pl.BlockSpec((1, d), lambda i: (0, 0))],
        out_specs=pl.BlockSpec((tm, d), lambda i: (i, 0)),
        out_shape=jax.ShapeDtypeStruct((s, d), F32),
        compiler_params=_params("parallel"),
        name="wo_norm_residual",
    )(m, wo, x, gain.reshape(1, d))


def _rotary_tables(s):
    pos = jnp.arange(s, dtype=F32)
    inv_freq = ROPE_THETA ** (-jnp.arange(0, ROT_DIM, 2, dtype=F32) / ROT_DIM)
    ang = pos[:, None] * inv_freq[None, :]
    return jnp.concatenate([jnp.cos(ang), jnp.sin(ang), jnp.zeros((s, HEAD_DIM - ROT_DIM), F32)], axis=1)


def _layer(x, norm_pre, w_in, b_merge, sink, conv_w, conv_b, wa, wb, wo, norm_post):
    s, d = x.shape
    n_q = d // HEAD_DIM
    n_kv = max(n_q // GROUP, 1)
    attn_w, kv_w = n_q * HEAD_DIM, n_kv * HEAD_DIM
    qkvg_cols = 2 * attn_w + 2 * kv_w
    rest_cols = w_in.shape[1] - qkvg_cols
    merge_a_col, merge_b_col = 4 * d, 5 * d

    h = _rmsnorm(x, norm_pre)
    assert kv_w <= 1024 and qkvg_cols % kv_w == 0, (kv_w, qkvg_cols)
    qkvg, wa16, wb16, wo16 = _in_proj(h, w_in, 0, qkvg_cols, kv_w, True, [wa, wb, wo], None,
                                      "in_proj_qkvg")
    tn_b, plan = _plan_rest(rest_cols, s // _tile(s, 1024, 8), n_kv, s // BLOCK)
    p_rest, za = _in_proj(h, w_in, qkvg_cols, rest_cols, tn_b, False, [],
                          (plan, qkvg, _rotary_tables(s), sink, n_q, n_kv), "in_proj_rest_attn")
    zb = _short_conv(p_rest, conv_w, conv_b, d, 0)
    m = _out_merge(za, zb, wa16, wb16, p_rest, b_merge.reshape(1, 2 * d), merge_a_col, merge_b_col)
    return _final(m, wo16, x, norm_post)


@jax.jit
def kernel(x, norm_pre, w_in, b_merge, attn_sink, conv_w, conv_b, w_attn_out, w_conv_out, w_out, norm_post):
    b, s, d = x.shape
    depth = norm_pre.shape[0]
    outs = []
    for bi in range(b):
        xb = x.reshape(s, d) if b == 1 else x[bi]
        for l in range(depth):
            xb = _layer(xb, norm_pre[l], w_in[l], b_merge[l], attn_sink[l], conv_w[l], conv_b[l],
                        w_attn_out[l], w_conv_out[l], w_out[l], norm_post[l])
        outs.append(xb)
    return outs[0].reshape(1, s, d) if b == 1 else jnp.stack(outs, axis=0)
```

```python
import functools
import math
from typing import NamedTuple, Optional

import jax
import jax.numpy as jnp
from jax import lax
from jax.experimental import pallas as pl
from jax.experimental.pallas import tpu as pltpu

HEAD_DIM = 128
GROUP = 4
WINDOW = 128
BLOCK = 128
ROPE_THETA = 500000.0
ROT_DIM = HEAD_DIM // 4
ROT_HALF = ROT_DIM // 2
RMS_EPS = 1e-6
LOG2E = 1.4426950408889634
LANES = 128
BF16_SUBLANES = 16
V7X_VMEM_BYTES = 64 * 1024 * 1024
VMEM_LIMIT_BYTES = V7X_VMEM_BYTES - 6 * 1024 * 1024

PROJ_ROWS, PROJ_COLS = 1024, 1024
MERGE_ROWS, MERGE_COLS = 1024, 512
FINAL_ROWS = 256
NORM_ROWS = 512
CONV_ROWS, CONV_COLS = 1024, 1024

F32 = jnp.float32
BF16 = jnp.bfloat16


def _tile(dim, pref, unit):
    t = min(pref, dim)
    t -= t % unit
    while t > unit and dim % t:
        t -= unit
    assert t >= unit and dim % t == 0, (dim, pref, unit)
    return t


def _params(*sem):
    return pltpu.CompilerParams(dimension_semantics=sem, vmem_limit_bytes=VMEM_LIMIT_BYTES)


def _rmsnorm_kernel(x_ref, g_ref, o_ref):
    x = x_ref[...]
    ms = jnp.mean(x * x, axis=-1, keepdims=True)
    o_ref[...] = (x * lax.rsqrt(ms + RMS_EPS) * g_ref[...]).astype(o_ref.dtype)


def _rmsnorm(x, gain):
    s, d = x.shape
    tm = _tile(s, NORM_ROWS, 8)
    return pl.pallas_call(
        _rmsnorm_kernel,
        grid=(s // tm,),
        in_specs=[pl.BlockSpec((tm, d), lambda i: (i, 0)),
                  pl.BlockSpec((1, d), lambda i: (0, 0))],
        out_specs=pl.BlockSpec((tm, d), lambda i: (i, 0)),
        out_shape=jax.ShapeDtypeStruct((s, d), BF16),
        compiler_params=_params("parallel"),
        name="rmsnorm_pre",
    )(x, gain.reshape(1, d))


class _AttnPlan(NamedTuple):
    nq: int
    units_per_head: int
    units: int


def _rotary_lanes(p):
    lane = lax.broadcasted_iota(jnp.int32, p.shape, 1)
    first, second = lane < ROT_HALF, (lane >= ROT_HALF) & (lane < ROT_DIM)
    c = jnp.where(first, p, jnp.where(second, pltpu.roll(p, ROT_HALF, 1), 1.0))
    s1 = jnp.where(first, -pltpu.roll(p, HEAD_DIM - ROT_HALF, 1), 0.0)
    s2 = jnp.where(second, p, 0.0)
    return c, s1, s2


def _rotate(t, tabs):
    c, s1, s2 = tabs
    return t * c + pltpu.roll(t, HEAD_DIM - ROT_HALF, 1) * s1 + pltpu.roll(t, ROT_HALF, 1) * s2


def _attn_unit(nq, hkv, t, nt, sink_ref, q_ref, kvp_ref, kvm_ref, kvn_ref,
               gate_ref, tp_ref, tm_ref, tn_ref, o_ref, s_scr, p_scr):
    rows = GROUP * BLOCK
    scale = HEAD_DIM ** -0.5
    kfull = jnp.concatenate([kvp_ref[0, 0], kvm_ref[0, 0], kvn_ref[0, 0]], axis=0)
    tab_m = _rotary_lanes(tm_ref[...])
    tabs = [jnp.concatenate(parts, axis=0)
            for parts in zip(_rotary_lanes(tp_ref[...]), tab_m, _rotary_lanes(tn_ref[...]))]
    kfull = _rotate(kfull.astype(F32), tabs).astype(kfull.dtype)
    vfull = jnp.concatenate([kvp_ref[1, 0], kvm_ref[1, 0], kvn_ref[1, 0]], axis=0)
    vext = jnp.concatenate([vfull, jnp.ones_like(vfull)], axis=1)

    for b in range(nq):
        q4 = q_ref[:, b * BLOCK:(b + 1) * BLOCK, :].reshape(rows, HEAD_DIM)
        tab_q = [jnp.concatenate([x[b * BLOCK:(b + 1) * BLOCK]] * GROUP, axis=0) for x in tab_m]
        q4 = _rotate(q4.astype(F32), tab_q).astype(q4.dtype)
        s_scr[b] = lax.dot_general(q4, kfull[b * BLOCK:(b + 3) * BLOCK], (((1,), (1,)), ((), ())),
                                   preferred_element_type=F32)

    qi = lax.broadcasted_iota(jnp.int32, (rows, BLOCK), 0) % BLOCK
    kc = lax.broadcasted_iota(jnp.int32, (rows, BLOCK), 1)
    no_prev = jnp.where(t > 0, 0, BLOCK)
    no_next = jnp.where(t < nt - 1, 0, BLOCK)
    sink_col = jnp.concatenate(
        [jnp.full((BLOCK, 1), sink_ref[hkv * GROUP + g], F32) for g in range(GROUP)], axis=0)
    sink_terms = []
    for b in range(nq):
        ok_prev = kc >= (qi + no_prev if b == 0 else qi)
        ok_next = kc <= (qi - no_next if b == nq - 1 else qi)
        s0 = jnp.where(ok_prev, s_scr[b, :, 0:BLOCK], -jnp.inf)
        s1 = s_scr[b, :, BLOCK:2 * BLOCK]
        s2 = jnp.where(ok_next, s_scr[b, :, 2 * BLOCK:3 * BLOCK], -jnp.inf)
        m_raw = jnp.max(jnp.maximum(jnp.maximum(s0, s1), s2), axis=-1, keepdims=True)
        m = jnp.maximum(m_raw * scale, sink_col)
        m2 = m * LOG2E
        for k, sk in enumerate((s0, s1, s2)):
            p_scr[b, :, k * BLOCK:(k + 1) * BLOCK] = jnp.exp2(sk * (scale * LOG2E) - m2).astype(p_scr.dtype)
        sink_terms.append(jnp.exp(sink_col - m))

    for b in range(nq):
        o2 = jnp.dot(p_scr[b], vext[b * BLOCK:(b + 3) * BLOCK], preferred_element_type=F32)
        num = o2[:, :HEAD_DIM]
        den = o2[:, HEAD_DIM:] + sink_terms[b]
        for g in range(GROUP):
            gate = gate_ref[g, b * BLOCK:(b + 1) * BLOCK, :].astype(F32)
            r = slice(g * BLOCK, (g + 1) * BLOCK)
            o_ref[b * BLOCK:(b + 1) * BLOCK, g * HEAD_DIM:(g + 1) * HEAD_DIM] = (
                (num[r] * gate) / (den[r] * (1.0 + jnp.exp(-gate)))).astype(o_ref.dtype)


def _attn_specs(plan, n_i, n_q, n_kv, nb):
    nq, upk, units = plan
    tq = nq * BLOCK
    assert (n_q + 2 * n_kv) % GROUP == 0, "attn_gate heads must start on a GROUP boundary"
    g0 = (n_q + 2 * n_kv) // GROUP
    kv_pair = GROUP // 2

    def unit(j, i):
        u = jnp.minimum(j * n_i + i, units - 1)
        return lax.div(u, upk), lax.rem(u, upk)

    def at(fn):
        return lambda j, i: fn(*unit(j, i))

    in_specs = [pl.BlockSpec(memory_space=pltpu.SMEM),
                pl.BlockSpec((GROUP, tq, HEAD_DIM), at(lambda h, t: (h, t, 0))),
                pl.BlockSpec((2, 1, BLOCK, HEAD_DIM),
                             at(lambda h, t: (kv_pair, h, jnp.maximum(t * nq - 1, 0), 0))),
                pl.BlockSpec((2, 1, tq, HEAD_DIM), at(lambda h, t: (kv_pair, h, t, 0))),
                pl.BlockSpec((2, 1, BLOCK, HEAD_DIM),
                             at(lambda h, t: (kv_pair, h, jnp.minimum((t + 1) * nq, nb - 1), 0))),
                pl.BlockSpec((GROUP, tq, HEAD_DIM), at(lambda h, t: (g0 + h, t, 0))),
                pl.BlockSpec((BLOCK, HEAD_DIM), at(lambda h, t: (jnp.maximum(t * nq - 1, 0), 0))),
                pl.BlockSpec((tq, HEAD_DIM), at(lambda h, t: (t, 0))),
                pl.BlockSpec((BLOCK, HEAD_DIM), at(lambda h, t: (jnp.minimum((t + 1) * nq, nb - 1), 0)))]
    out_spec = pl.BlockSpec((tq, GROUP * HEAD_DIM), at(lambda h, t: (t, h)))
    scratch = [pltpu.VMEM((nq, GROUP * BLOCK, 3 * BLOCK), F32),
               pltpu.VMEM((nq, GROUP * BLOCK, 3 * BLOCK), BF16)]
    return in_specs, out_spec, scratch


class _ProjCfg(NamedTuple):
    head_major: bool
    n_riders: int
    kc: int
    col0: int
    attn: Optional[_AttnPlan]


N_ATTN_IN = 9


def _in_proj_kernel(cfg, h_ref, w_hbm, *refs):
    refs = list(refs)
    take = lambda n: [refs.pop(0) for _ in range(n)]
    riders_in = take(cfg.n_riders)
    attn_in = take(N_ATTN_IN if cfg.attn else 0)
    o_ref, = take(1)
    riders_out = take(cfg.n_riders)
    z_ref = take(1 if cfg.attn else 0)
    w_even, w_odd, stage, sem = take(4)
    attn_scratch = refs
    kc = cfg.kc
    j, i = pl.program_id(0), pl.program_id(1)
    n_j, n_i = pl.num_programs(0), pl.num_programs(1)
    tn = w_even.shape[1]
    step = j * n_i + i
    slot = step % 2

    def chunk_copy(jb, ic, to_slot):
        return pltpu.make_async_copy(
            w_hbm.at[pl.ds(ic * kc, kc), pl.ds(cfg.col0 + jb * tn, tn)],
            stage.at[to_slot], sem.at[to_slot])

    j_next = jnp.minimum(j + 1, n_j - 1)

    @pl.when(step == 0)
    def _():
        n_chunks = w_even.shape[0] // kc
        for ic in range(min(2, n_chunks)):
            chunk_copy(0, ic, ic).start()
        for ic in range(n_chunks):
            chunk_copy(0, ic, ic % 2).wait()
            w_even[ic * kc:(ic + 1) * kc, :] = stage[ic % 2].astype(BF16)
            if ic + 2 < n_chunks:
                chunk_copy(0, ic + 2, ic % 2).start()
        chunk_copy(j_next, 0, 0).start()

    @pl.when(step + 1 < n_j * n_i)
    def _():
        wrap = i + 1 == n_i
        chunk_copy(jnp.minimum(jnp.where(wrap, j + 2, j + 1), n_j - 1),
                   jnp.where(wrap, 0, i + 1), 1 - slot).start()

    chunk_copy(j_next, i, slot).wait()

    def compute(w_cur, w_next):
        for src, dst in zip(riders_in, riders_out):
            dst[...] = src[...].astype(dst.dtype)
        acc = jnp.dot(h_ref[...], w_cur[...], preferred_element_type=F32)
        if cfg.head_major:
            for g in range(o_ref.shape[0]):
                o_ref[g] = acc[:, g * HEAD_DIM:(g + 1) * HEAD_DIM].astype(o_ref.dtype)
        else:
            o_ref[...] = acc.astype(o_ref.dtype)
        w_next[pl.ds(pl.multiple_of(i * kc, kc), kc), :] = stage[slot].astype(BF16)
        if cfg.attn:
            unit = jnp.minimum(step, cfg.attn.units - 1)
            _attn_unit(cfg.attn.nq, unit // cfg.attn.units_per_head, unit % cfg.attn.units_per_head,
                       cfg.attn.units_per_head, *attn_in, *z_ref, *attn_scratch)

    pl.when(j % 2 == 0)(lambda: compute(w_even, w_odd))
    pl.when(j % 2 == 1)(lambda: compute(w_odd, w_even))


def _plan_rest(n_cols, n_i, n_kv, nb):
    best = None
    for tn in range(PROJ_COLS, 0, -LANES):
        if n_cols % tn:
            continue
        steps = (n_cols // tn) * n_i
        for nq in (1, 2, 4, 8):
            if nb % nq or n_kv * (nb // nq) > steps:
                continue
            waste = steps - n_kv * (nb // nq)
            if best is None or waste < best[0]:
                best = (waste, tn, _AttnPlan(nq, nb // nq, n_kv * (nb // nq)))
            break
    assert best is not None, (n_cols, n_i, n_kv, nb)
    return best[1], best[2]


def _in_proj(h, w_in, col0, n_cols, tn, head_major, riders, attn, name):
    s, d = h.shape
    tm = _tile(s, PROJ_ROWS, 8)
    n_j, n_i = n_cols // tn, s // tm
    kc = d // n_i
    assert n_cols % tn == 0 and d % n_i == 0 and kc % BF16_SUBLANES == 0, (n_cols, tn, d, n_i)

    in_specs = [pl.BlockSpec((tm, d), lambda j, i: (i, 0)),
                pl.BlockSpec(memory_space=pl.ANY)]
    args = [h, w_in]
    if head_major:
        assert tn % HEAD_DIM == 0
        out_specs = [pl.BlockSpec((tn // HEAD_DIM, tm, HEAD_DIM), lambda j, i: (j, i, 0))]
        out_shape = [jax.ShapeDtypeStruct((n_cols // HEAD_DIM, s, HEAD_DIM), BF16)]
    else:
        out_specs = [pl.BlockSpec((tm, tn), lambda j, i: (i, j))]
        out_shape = [jax.ShapeDtypeStruct((s, n_cols), BF16)]

    for r in riders:
        rows, cols = r.shape
        rr = next(c for c in range(BF16_SUBLANES, rows + 1, BF16_SUBLANES)
                  if rows % c == 0 and rows // c <= n_j * n_i)
        spec = pl.BlockSpec((rr, cols), functools.partial(
            lambda j, i, last: (jnp.minimum(j * n_i + i, last), 0), last=rows // rr - 1))
        in_specs.append(spec)
        args.append(r)
        out_specs.append(spec)
        out_shape.append(jax.ShapeDtypeStruct(r.shape, BF16))

    scratch = [pltpu.VMEM((d, tn), BF16), pltpu.VMEM((d, tn), BF16),
               pltpu.VMEM((2, kc, tn), F32), pltpu.SemaphoreType.DMA((2,))]
    plan = None
    if attn is not None:
        plan, qkvg, tabs, sink, n_q, n_kv = attn
        a_in, a_out, a_scratch = _attn_specs(plan, n_i, n_q, n_kv, s // BLOCK)
        in_specs += a_in
        kv4 = qkvg.reshape(-1, n_kv, s, HEAD_DIM)
        args += [sink, qkvg, kv4, kv4, kv4, qkvg] + [tabs] * 3
        out_specs.append(a_out)
        out_shape.append(jax.ShapeDtypeStruct((s, n_q * HEAD_DIM), BF16))
        scratch += a_scratch

    cfg = _ProjCfg(head_major, len(riders), kc, col0, plan)
    return pl.pallas_call(
        functools.partial(_in_proj_kernel, cfg),
        grid=(n_j, n_i),
        in_specs=in_specs,
        out_specs=out_specs,
        out_shape=out_shape,
        scratch_shapes=scratch,
        compiler_params=_params("arbitrary", "arbitrary"),
        name=name,
    )(*args)


def _conv_block(i, ni, bg_ref, cg_ref, cx_ref, gt_ref, cgp_ref, cxp_ref, cgn_ref, cxn_ref, w_ref, b_ref):
    tm = bg_ref.shape[0]
    halo = cgp_ref.shape[0]
    u = cg_ref[...].astype(F32) * cx_ref[...].astype(F32)
    u_prev = cgp_ref[halo - 1:halo, :].astype(F32) * cxp_ref[halo - 1:halo, :].astype(F32)
    u_next = cgn_ref[0:1, :].astype(F32) * cxn_ref[0:1, :].astype(F32)
    u_prev = jnp.where(i > 0, u_prev, 0.0)
    u_next = jnp.where(i < ni - 1, u_next, 0.0)
    row = lax.broadcasted_iota(jnp.int32, u.shape, 0)
    up = jnp.where(row == 0, u_prev, pltpu.roll(u, 1, 0))
    dn = jnp.where(row == tm - 1, u_next, pltpu.roll(u, tm - 1, 0))
    c = up * w_ref[0:1, :] + u * w_ref[1:2, :] + dn * w_ref[2:3, :] + b_ref[...]
    gate = gt_ref[...].astype(F32)
    return bg_ref[...].astype(F32) * c * (gate * jax.nn.sigmoid(gate))


def _conv_kernel(*refs):
    o_ref = refs[-1]
    o_ref[...] = _conv_block(pl.program_id(0), pl.num_programs(0), *refs[:-1]).astype(o_ref.dtype)


def _short_conv(p_rest, conv_w, conv_b, d, col0):
    s = p_rest.shape[0]
    halo = BF16_SUBLANES
    tm = _tile(s, CONV_ROWS, halo)
    tc = _tile(math.gcd(d, col0), CONV_COLS, LANES)
    nc = d // tc
    hb = tm // halo
    last_hb = s // halo - 1

    def cur(k):
        return pl.BlockSpec((tm, tc), lambda i, c: (i, (col0 // tc) + k * nc + c))

    def prev(k):
        return pl.BlockSpec((halo, tc),
                            lambda i, c: (jnp.maximum(i * hb - 1, 0), (col0 // tc) + k * nc + c))

    def nxt(k):
        return pl.BlockSpec((halo, tc),
                            lambda i, c: (jnp.minimum((i + 1) * hb, last_hb), (col0 // tc) + k * nc + c))

    return pl.pallas_call(
        _conv_kernel,
        grid=(s // tm, nc),
        in_specs=[cur(0), cur(1), cur(2), cur(3), prev(1), prev(2), nxt(1), nxt(2),
                  pl.BlockSpec((3, tc), lambda i, c: (0, c)),
                  pl.BlockSpec((1, tc), lambda i, c: (0, c))],
        out_specs=pl.BlockSpec((tm, tc), lambda i, c: (i, c)),
        out_shape=jax.ShapeDtypeStruct((s, d), BF16),
        compiler_params=_params("parallel", "parallel"),
        name="short_conv",
    )(p_rest, p_rest, p_rest, p_rest, p_rest, p_rest, p_rest, p_rest, conv_w, conv_b.reshape(1, d))


def _merge_kernel(za_ref, zb_ref, wa_ref, wb_ref, la_ref, lb_ref, ba_ref, bb_ref, o_ref):
    ya = jnp.dot(za_ref[...], wa_ref[...], preferred_element_type=F32)
    yb = jnp.dot(zb_ref[...], wb_ref[...], preferred_element_type=F32)
    ga = jax.nn.sigmoid(la_ref[...].astype(F32) + ba_ref[...])
    gb = jax.nn.sigmoid(lb_ref[...].astype(F32) + bb_ref[...])
    o_ref[...] = (ga * ya + gb * yb).astype(o_ref.dtype)


def _out_merge(za, zb, wa, wb, p_rest, b_merge, col_a, col_b):
    s, d = za.shape
    tm = _tile(s, MERGE_ROWS, 8)
    tn = _tile(math.gcd(d, col_a), MERGE_COLS, LANES)
    nn = d // tn
    return pl.pallas_call(
        _merge_kernel,
        grid=(s // tm, nn),
        in_specs=[pl.BlockSpec((tm, d), lambda i, j: (i, 0)),
                  pl.BlockSpec((tm, d), lambda i, j: (i, 0)),
                  pl.BlockSpec((d, tn), lambda i, j: (0, j)),
                  pl.BlockSpec((d, tn), lambda i, j: (0, j)),
                  pl.BlockSpec((tm, tn), lambda i, j: (i, col_a // tn + j)),
                  pl.BlockSpec((tm, tn), lambda i, j: (i, col_b // tn + j)),
                  pl.BlockSpec((1, tn), lambda i, j: (0, j)),
                  pl.BlockSpec((1, tn), lambda i, j: (0, nn + j))],
        out_specs=pl.BlockSpec((tm, tn), lambda i, j: (i, j)),
        out_shape=jax.ShapeDtypeStruct((s, d), BF16),
        compiler_params=_params("parallel", "parallel"),
        name="out_proj_merge",
    )(za, zb, wa, wb, p_rest, p_rest, b_merge, b_merge)


def _final_kernel(m_ref, w_ref, x_ref, g_ref, o_ref):
    o = jnp.dot(m_ref[...], w_ref[...], preferred_element_type=F32)
    ms = jnp.mean(o * o, axis=-1, keepdims=True)
    o_ref[...] = x_ref[...] + o * lax.rsqrt(ms + RMS_EPS) * g_ref[...]


def _final(m, wo, x, gain):
    s, d = m.shape
    tm = _tile(s, FINAL_ROWS, 8)
    return pl.pallas_call(
        _final_kernel,
        grid=(s // tm,),
        in_specs=[pl.BlockSpec((tm, d), lambda i: (i, 0)),
                  pl.BlockSpec((d, d), lambda i: (0, 0), pipeline_mode=pl.Buffered(1)),
                  pl.BlockSpec((tm, d), lambda i: (i, 0)),
                  pl.BlockSpec((1, d), lambda i: (0, 0))],
        out_specs=pl.BlockSpec((tm, d), lambda i: (i, 0)),
        out_shape=jax.ShapeDtypeStruct((s, d), F32),
        compiler_params=_params("parallel"),
        name="wo_norm_residual",
    )(m, wo, x, gain.reshape(1, d))


def _rotary_tables(s):
    pos = jnp.arange(s, dtype=F32)
    inv_freq = ROPE_THETA ** (-jnp.arange(0, ROT_DIM, 2, dtype=F32) / ROT_DIM)
    ang = pos[:, None] * inv_freq[None, :]
    return jnp.concatenate([jnp.cos(ang), jnp.sin(ang), jnp.zeros((s, HEAD_DIM - ROT_DIM), F32)], axis=1)


def _layer(x, norm_pre, w_in, b_merge, sink, conv_w, conv_b, wa, wb, wo, norm_post):
    s, d = x.shape
    n_q = d // HEAD_DIM
    n_kv = max(n_q // GROUP, 1)
    attn_w, kv_w = n_q * HEAD_DIM, n_kv * HEAD_DIM
    qkvg_cols = 2 * attn_w + 2 * kv_w
    rest_cols = w_in.shape[1] - qkvg_cols
    merge_a_col, merge_b_col = 4 * d, 5 * d

    h = _rmsnorm(x, norm_pre)
    assert kv_w <= PROJ_COLS and qkvg_cols % kv_w == 0, (kv_w, qkvg_cols)
    qkvg, wa16, wb16, wo16 = _in_proj(h, w_in, 0, qkvg_cols, kv_w, True, [wa, wb, wo], None,
                                      "in_proj_qkvg")
    tn_b, plan = _plan_rest(rest_cols, s // _tile(s, PROJ_ROWS, 8), n_kv, s // BLOCK)
    p_rest, za = _in_proj(h, w_in, qkvg_cols, rest_cols, tn_b, False, [],
                          (plan, qkvg, _rotary_tables(s), sink, n_q, n_kv), "in_proj_rest_attn")
    zb = _short_conv(p_rest, conv_w, conv_b, d, 0)
    m = _out_merge(za, zb, wa16, wb16, p_rest, b_merge.reshape(1, 2 * d), merge_a_col, merge_b_col)
    return _final(m, wo16, x, norm_post)


@jax.jit
def kernel(x, norm_pre, w_in, b_merge, attn_sink, conv_w, conv_b, w_attn_out, w_conv_out, w_out, norm_post):
    b, s, d = x.shape
    depth = norm_pre.shape[0]
    outs = []
    for bi in range(b):
        xb = x.reshape(s, d) if b == 1 else x[bi]
        for l in range(depth):
            xb = _layer(xb, norm_pre[l], w_in[l], b_merge[l], attn_sink[l], conv_w[l], conv_b[l],
                        w_attn_out[l], w_conv_out[l], w_out[l], norm_post[l])
        outs.append(xb)
    return outs[0].reshape(1, s, d) if b == 1 else jnp.stack(outs, axis=0)
```

```python
import functools
import math
from typing import NamedTuple, Optional

import jax
import jax.numpy as jnp
from jax import lax
from jax.experimental import pallas as pl
from jax.experimental.pallas import tpu as pltpu

HEAD_DIM = 128
GROUP = 4
WINDOW = 128
BLOCK = 128
ROPE_THETA = 500000.0
ROT_DIM = HEAD_DIM // 4
ROT_HALF = ROT_DIM // 2
RMS_EPS = 1e-6
LOG2E = 1.4426950408889634
LANES = 128
BF16_SUBLANES = 16
V7X_VMEM_BYTES = 64 * 1024 * 1024
VMEM_LIMIT_BYTES = V7X_VMEM_BYTES - 6 * 1024 * 1024

PROJ_ROWS, PROJ_COLS = 1024, 1024
MERGE_ROWS, MERGE_COLS = 1024, 512
FINAL_ROWS = 256
NORM_ROWS = 512
CONV_ROWS, CONV_COLS = 1024, 1024

F32 = jnp.float32
BF16 = jnp.bfloat16


def _tile(dim, pref, unit):
    t = min(pref, dim)
    t -= t % unit
    while t > unit and dim % t:
        t -= unit
    assert t >= unit and dim % t == 0, (dim, pref, unit)
    return t


def _params(*sem):
    return pltpu.CompilerParams(dimension_semantics=sem, vmem_limit_bytes=VMEM_LIMIT_BYTES)


def _rmsnorm_kernel(x_ref, g_ref, o_ref):
    x = x_ref[...]
    ms = jnp.mean(x * x, axis=-1, keepdims=True)
    o_ref[...] = (x * lax.rsqrt(ms + RMS_EPS) * g_ref[...]).astype(o_ref.dtype)


def _rmsnorm(x, gain):
    s, d = x.shape
    tm = _tile(s, NORM_ROWS, 8)
    return pl.pallas_call(
        _rmsnorm_kernel,
        grid=(s // tm,),
        in_specs=[pl.BlockSpec((tm, d), lambda i: (i, 0)),
                  pl.BlockSpec((1, d), lambda i: (0, 0))],
        out_specs=pl.BlockSpec((tm, d), lambda i: (i, 0)),
        out_shape=jax.ShapeDtypeStruct((s, d), BF16),
        compiler_params=_params("parallel"),
        name="rmsnorm_pre",
    )(x, gain.reshape(1, d))


class _AttnPlan(NamedTuple):
    nq: int
    units_per_head: int
    units: int


def _rotary_lanes(p):
    lane = lax.broadcasted_iota(jnp.int32, p.shape, 1)
    first, second = lane < ROT_HALF, (lane >= ROT_HALF) & (lane < ROT_DIM)
    c = jnp.where(first, p, jnp.where(second, pltpu.roll(p, ROT_HALF, 1), 1.0))
    s1 = jnp.where(first, -pltpu.roll(p, HEAD_DIM - ROT_HALF, 1), 0.0)
    s2 = jnp.where(second, p, 0.0)
    return c, s1, s2


def _rotate(t, tabs):
    c, s1, s2 = tabs
    return t * c + pltpu.roll(t, HEAD_DIM - ROT_HALF, 1) * s1 + pltpu.roll(t, ROT_HALF, 1) * s2


def _attn_unit(nq, hkv, t, nt, sink_ref, q_ref, kvp_ref, kvm_ref, kvn_ref,
               gate_ref, tp_ref, tm_ref, tn_ref, o_ref, s_scr, p_scr):
    rows = GROUP * BLOCK
    scale = HEAD_DIM ** -0.5
    kfull = jnp.concatenate([kvp_ref[0, 0], kvm_ref[0, 0], kvn_ref[0, 0]], axis=0)
    tab_m = _rotary_lanes(tm_ref[...])
    tabs = [jnp.concatenate(parts, axis=0)
            for parts in zip(_rotary_lanes(tp_ref[...]), tab_m, _rotary_lanes(tn_ref[...]))]
    kfull = _rotate(kfull.astype(F32), tabs).astype(kfull.dtype)
    vfull = jnp.concatenate([kvp_ref[1, 0], kvm_ref[1, 0], kvn_ref[1, 0]], axis=0)
    vext = jnp.concatenate([vfull, jnp.ones_like(vfull)], axis=1)

    for b in range(nq):
        q4 = q_ref[:, b * BLOCK:(b + 1) * BLOCK, :].reshape(rows, HEAD_DIM)
        tab_q = [jnp.concatenate([x[b * BLOCK:(b + 1) * BLOCK]] * GROUP, axis=0) for x in tab_m]
        q4 = _rotate(q4.astype(F32), tab_q).astype(q4.dtype)
        s_scr[b] = lax.dot_general(q4, kfull[b * BLOCK:(b + 3) * BLOCK], (((1,), (1,)), ((), ())),
                                   preferred_element_type=F32)

    qi = lax.broadcasted_iota(jnp.int32, (rows, BLOCK), 0) % BLOCK
    kc = lax.broadcasted_iota(jnp.int32, (rows, BLOCK), 1)
    no_prev = jnp.where(t > 0, 0, BLOCK)
    no_next = jnp.where(t < nt - 1, 0, BLOCK)
    sink_col = jnp.concatenate(
        [jnp.full((BLOCK, 1), sink_ref[hkv * GROUP + g], F32) for g in range(GROUP)], axis=0)
    sink_terms = []
    for b in range(nq):
        ok_prev = kc >= (qi + no_prev if b == 0 else qi)
        ok_next = kc <= (qi - no_next if b == nq - 1 else qi)
        s0 = jnp.where(ok_prev, s_scr[b, :, 0:BLOCK], -jnp.inf)
        s1 = s_scr[b, :, BLOCK:2 * BLOCK]
        s2 = jnp.where(ok_next, s_scr[b, :, 2 * BLOCK:3 * BLOCK], -jnp.inf)
        m_raw = jnp.max(jnp.maximum(jnp.maximum(s0, s1), s2), axis=-1, keepdims=True)
        m = jnp.maximum(m_raw * scale, sink_col)
        m2 = m * LOG2E
        for k, sk in enumerate((s0, s1, s2)):
            p_scr[b, :, k * BLOCK:(k + 1) * BLOCK] = jnp.exp2(sk * (scale * LOG2E) - m2).astype(p_scr.dtype)
        sink_terms.append(jnp.exp(sink_col - m))

    for b in range(nq):
        o2 = jnp.dot(p_scr[b], vext[b * BLOCK:(b + 3) * BLOCK], preferred_element_type=F32)
        num = o2[:, :HEAD_DIM]
        den = o2[:, HEAD_DIM:] + sink_terms[b]
        for g in range(GROUP):
            gate = gate_ref[g, b * BLOCK:(b + 1) * BLOCK, :].astype(F32)
            r = slice(g * BLOCK, (g + 1) * BLOCK)
            o_ref[b * BLOCK:(b + 1) * BLOCK, g * HEAD_DIM:(g + 1) * HEAD_DIM] = (
                (num[r] * gate) / (den[r] * (1.0 + jnp.exp(-gate)))).astype(o_ref.dtype)


def _attn_specs(plan, n_i, n_q, n_kv, nb):
    nq, upk, units = plan
    tq = nq * BLOCK
    assert (n_q + 2 * n_kv) % GROUP == 0, "attn_gate heads must start on a GROUP boundary"
    g0 = (n_q + 2 * n_kv) // GROUP
    kv_pair = GROUP // 2

    def unit(j, i):
        u = jnp.minimum(j * n_i + i, units - 1)
        return lax.div(u, upk), lax.rem(u, upk)

    def at(fn):
        return lambda j, i: fn(*unit(j, i))

    in_specs = [pl.BlockSpec(memory_space=pltpu.SMEM),
                pl.BlockSpec((GROUP, tq, HEAD_DIM), at(lambda h, t: (h, t, 0))),
                pl.BlockSpec((2, 1, BLOCK, HEAD_DIM),
                             at(lambda h, t: (kv_pair, h, jnp.maximum(t * nq - 1, 0), 0))),
                pl.BlockSpec((2, 1, tq, HEAD_DIM), at(lambda h, t: (kv_pair, h, t, 0))),
                pl.BlockSpec((2, 1, BLOCK, HEAD_DIM),
                             at(lambda h, t: (kv_pair, h, jnp.minimum((t + 1) * nq, nb - 1), 0))),
                pl.BlockSpec((GROUP, tq, HEAD_DIM), at(lambda h, t: (g0 + h, t, 0))),
                pl.BlockSpec((BLOCK, HEAD_DIM), at(lambda h, t: (jnp.maximum(t * nq - 1, 0), 0))),
                pl.BlockSpec((tq, HEAD_DIM), at(lambda h, t: (t, 0))),
                pl.BlockSpec((BLOCK, HEAD_DIM), at(lambda h, t: (jnp.minimum((t + 1) * nq, nb - 1), 0)))]
    out_spec = pl.BlockSpec((tq, GROUP * HEAD_DIM), at(lambda h, t: (t, h)))
    scratch = [pltpu.VMEM((nq, GROUP * BLOCK, 3 * BLOCK), F32),
               pltpu.VMEM((nq, GROUP * BLOCK, 3 * BLOCK), BF16)]
    return in_specs, out_spec, scratch


class _ProjCfg(NamedTuple):
    head_major: bool
    n_riders: int
    kc: int
    col0: int
    attn: Optional[_AttnPlan]


N_ATTN_IN = 9


def _in_proj_kernel(cfg, h_ref, w_hbm, *refs):
    refs = list(refs)
    take = lambda n: [refs.pop(0) for _ in range(n)]
    riders_in = take(cfg.n_riders)
    attn_in = take(N_ATTN_IN if cfg.attn else 0)
    o_ref, = take(1)
    riders_out = take(cfg.n_riders)
    z_ref = take(1 if cfg.attn else 0)
    w_copies, stage, sem = take(3)
    attn_scratch = refs
    kc = cfg.kc
    j, i = pl.program_id(0), pl.program_id(1)
    n_j, n_i = pl.num_programs(0), pl.num_programs(1)
    tn = w_copies.shape[2]
    step = j * n_i + i
    slot = step % 2

    def chunk_copy(jb, ic, to_slot):
        return pltpu.make_async_copy(
            w_hbm.at[pl.ds(ic * kc, kc), pl.ds(cfg.col0 + jb * tn, tn)],
            stage.at[to_slot], sem.at[to_slot])

    j_next = jnp.minimum(j + 1, n_j - 1)

    @pl.when(step == 0)
    def _():
        n_chunks = w_copies.shape[1] // kc
        for ic in range(min(2, n_chunks)):
            chunk_copy(0, ic, ic).start()
        for ic in range(n_chunks):
            chunk_copy(0, ic, ic % 2).wait()
            w_copies[0, ic * kc:(ic + 1) * kc, :] = stage[ic % 2].astype(BF16)
            if ic + 2 < n_chunks:
                chunk_copy(0, ic + 2, ic % 2).start()
        chunk_copy(j_next, 0, 0).start()

    @pl.when(step + 1 < n_j * n_i)
    def _():
        wrap = i + 1 == n_i
        chunk_copy(jnp.minimum(jnp.where(wrap, j + 2, j + 1), n_j - 1),
                   jnp.where(wrap, 0, i + 1), 1 - slot).start()

    chunk_copy(j_next, i, slot).wait()

    cur = j % 2
    for src, dst in zip(riders_in, riders_out):
        dst[...] = src[...].astype(dst.dtype)
    acc = jnp.dot(h_ref[...], w_copies[cur], preferred_element_type=F32)
    if cfg.head_major:
        for g in range(o_ref.shape[0]):
            o_ref[g] = acc[:, g * HEAD_DIM:(g + 1) * HEAD_DIM].astype(o_ref.dtype)
    else:
        o_ref[...] = acc.astype(o_ref.dtype)
    w_copies[1 - cur, pl.ds(pl.multiple_of(i * kc, kc), kc), :] = stage[slot].astype(BF16)
    if cfg.attn:
        unit = jnp.minimum(step, cfg.attn.units - 1)
        upk = cfg.attn.units_per_head
        _attn_unit(cfg.attn.nq, lax.div(unit, upk), lax.rem(unit, upk), upk,
                   *attn_in, *z_ref, *attn_scratch)


def _plan_rest(n_cols, n_i, n_kv, nb):
    best = None
    for tn in range(PROJ_COLS, 0, -LANES):
        if n_cols % tn:
            continue
        steps = (n_cols // tn) * n_i
        for nq in (1, 2, 4, 8):
            if nb % nq or n_kv * (nb // nq) > steps:
                continue
            waste = steps - n_kv * (nb // nq)
            if best is None or waste < best[0]:
                best = (waste, tn, _AttnPlan(nq, nb // nq, n_kv * (nb // nq)))
            break
    assert best is not None, (n_cols, n_i, n_kv, nb)
    return best[1], best[2]


def _in_proj(h, w_in, col0, n_cols, tn, head_major, riders, attn, name):
    s, d = h.shape
    tm = _tile(s, PROJ_ROWS, 8)
    n_j, n_i = n_cols // tn, s // tm
    kc = d // n_i
    assert n_cols % tn == 0 and d % n_i == 0 and kc % BF16_SUBLANES == 0, (n_cols, tn, d, n_i)

    in_specs = [pl.BlockSpec((tm, d), lambda j, i: (i, 0)),
                pl.BlockSpec(memory_space=pl.ANY)]
    args = [h, w_in]
    if head_major:
        assert tn % HEAD_DIM == 0
        out_specs = [pl.BlockSpec((tn // HEAD_DIM, tm, HEAD_DIM), lambda j, i: (j, i, 0))]
        out_shape = [jax.ShapeDtypeStruct((n_cols // HEAD_DIM, s, HEAD_DIM), BF16)]
    else:
        out_specs = [pl.BlockSpec((tm, tn), lambda j, i: (i, j))]
        out_shape = [jax.ShapeDtypeStruct((s, n_cols), BF16)]

    for r in riders:
        rows, cols = r.shape
        rr = next(c for c in range(BF16_SUBLANES, rows + 1, BF16_SUBLANES)
                  if rows % c == 0 and rows // c <= n_j * n_i)
        spec = pl.BlockSpec((rr, cols), functools.partial(
            lambda j, i, last: (jnp.minimum(j * n_i + i, last), 0), last=rows // rr - 1))
        in_specs.append(spec)
        args.append(r)
        out_specs.append(spec)
        out_shape.append(jax.ShapeDtypeStruct(r.shape, BF16))

    scratch = [pltpu.VMEM((2, d, tn), BF16),
               pltpu.VMEM((2, kc, tn), F32), pltpu.SemaphoreType.DMA((2,))]
    plan = None
    if attn is not None:
        plan, qkvg, tabs, sink, n_q, n_kv = attn
        a_in, a_out, a_scratch = _attn_specs(plan, n_i, n_q, n_kv, s // BLOCK)
        in_specs += a_in
        kv4 = qkvg.reshape(-1, n_kv, s, HEAD_DIM)
        args += [sink, qkvg, kv4, kv4, kv4, qkvg] + [tabs] * 3
        out_specs.append(a_out)
        out_shape.append(jax.ShapeDtypeStruct((s, n_q * HEAD_DIM), BF16))
        scratch += a_scratch

    cfg = _ProjCfg(head_major, len(riders), kc, col0, plan)
    return pl.pallas_call(
        functools.partial(_in_proj_kernel, cfg),
        grid=(n_j, n_i),
        in_specs=in_specs,
        out_specs=out_specs,
        out_shape=out_shape,
        scratch_shapes=scratch,
        compiler_params=_params("arbitrary", "arbitrary"),
        name=name,
    )(*args)


def _conv_block(i, ni, bg_ref, cg_ref, cx_ref, gt_ref, cgp_ref, cxp_ref, cgn_ref, cxn_ref, w_ref, b_ref):
    tm = bg_ref.shape[0]
    halo = cgp_ref.shape[0]
    u = cg_ref[...].astype(F32) * cx_ref[...].astype(F32)
    u_prev = cgp_ref[halo - 1:halo, :].astype(F32) * cxp_ref[halo - 1:halo, :].astype(F32)
    u_next = cgn_ref[0:1, :].astype(F32) * cxn_ref[0:1, :].astype(F32)
    u_prev = jnp.where(i > 0, u_prev, 0.0)
    u_next = jnp.where(i < ni - 1, u_next, 0.0)
    row = lax.broadcasted_iota(jnp.int32, u.shape, 0)
    up = jnp.where(row == 0, u_prev, pltpu.roll(u, 1, 0))
    dn = jnp.where(row == tm - 1, u_next, pltpu.roll(u, tm - 1, 0))
    c = up * w_ref[0:1, :] + u * w_ref[1:2, :] + dn * w_ref[2:3, :] + b_ref[...]
    gate = gt_ref[...].astype(F32)
    return bg_ref[...].astype(F32) * c * (gate * jax.nn.sigmoid(gate))


def _conv_kernel(*refs):
    o_ref = refs[-1]
    o_ref[...] = _conv_block(pl.program_id(0), pl.num_programs(0), *refs[:-1]).astype(o_ref.dtype)


def _short_conv(p_rest, conv_w, conv_b, d, col0):
    s = p_rest.shape[0]
    halo = BF16_SUBLANES
    tm = _tile(s, CONV_ROWS, halo)
    tc = _tile(math.gcd(d, col0), CONV_COLS, LANES)
    nc = d // tc
    hb = tm // halo
    last_hb = s // halo - 1

    def cur(k):
        return pl.BlockSpec((tm, tc), lambda i, c: (i, (col0 // tc) + k * nc + c))

    def prev(k):
        return pl.BlockSpec((halo, tc),
                            lambda i, c: (jnp.maximum(i * hb - 1, 0), (col0 // tc) + k * nc + c))

    def nxt(k):
        return pl.BlockSpec((halo, tc),
                            lambda i, c: (jnp.minimum((i + 1) * hb, last_hb), (col0 // tc) + k * nc + c))

    return pl.pallas_call(
        _conv_kernel,
        grid=(s // tm, nc),
        in_specs=[cur(0), cur(1), cur(2), cur(3), prev(1), prev(2), nxt(1), nxt(2),
                  pl.BlockSpec((3, tc), lambda i, c: (0, c)),
                  pl.BlockSpec((1, tc), lambda i, c: (0, c))],
        out_specs=pl.BlockSpec((tm, tc), lambda i, c: (i, c)),
        out_shape=jax.ShapeDtypeStruct((s, d), BF16),
        compiler_params=_params("parallel", "parallel"),
        name="short_conv",
    )(p_rest, p_rest, p_rest, p_rest, p_rest, p_rest, p_rest, p_rest, conv_w, conv_b.reshape(1, d))


def _merge_kernel(za_ref, zb_ref, wa_ref, wb_ref, la_ref, lb_ref, ba_ref, bb_ref, o_ref):
    ya = jnp.dot(za_ref[...], wa_ref[...], preferred_element_type=F32)
    yb = jnp.dot(zb_ref[...], wb_ref[...], preferred_element_type=F32)
    ga = jax.nn.sigmoid(la_ref[...].astype(F32) + ba_ref[...])
    gb = jax.nn.sigmoid(lb_ref[...].astype(F32) + bb_ref[...])
    o_ref[...] = (ga * ya + gb * yb).astype(o_ref.dtype)


def _out_merge(za, zb, wa, wb, p_rest, b_merge, col_a, col_b):
    s, d = za.shape
    tm = _tile(s, MERGE_ROWS, 8)
    tn = _tile(math.gcd(d, col_a), MERGE_COLS, LANES)
    nn = d // tn
    return pl.pallas_call(
        _merge_kernel,
        grid=(s // tm, nn),
        in_specs=[pl.BlockSpec((tm, d), lambda i, j: (i, 0)),
                  pl.BlockSpec((tm, d), lambda i, j: (i, 0)),
                  pl.BlockSpec((d, tn), lambda i, j: (0, j)),
                  pl.BlockSpec((d, tn), lambda i, j: (0, j)),
                  pl.BlockSpec((tm, tn), lambda i, j: (i, col_a // tn + j)),
                  pl.BlockSpec((tm, tn), lambda i, j: (i, col_b // tn + j)),
                  pl.BlockSpec((1, tn), lambda i, j: (0, j)),
                  pl.BlockSpec((1, tn), lambda i, j: (0, nn + j))],
        out_specs=pl.BlockSpec((tm, tn), lambda i, j: (i, j)),
        out_shape=jax.ShapeDtypeStruct((s, d), BF16),
        compiler_params=_params("parallel", "parallel"),
        name="out_proj_merge",
    )(za, zb, wa, wb, p_rest, p_rest, b_merge, b_merge)


def _final_kernel(m_ref, w_ref, x_ref, g_ref, o_ref):
    o = jnp.dot(m_ref[...], w_ref[...], preferred_element_type=F32)
    ms = jnp.mean(o * o, axis=-1, keepdims=True)
    o_ref[...] = x_ref[...] + o * lax.rsqrt(ms + RMS_EPS) * g_ref[...]


def _final(m, wo, x, gain):
    s, d = m.shape
    tm = _tile(s, FINAL_ROWS, 8)
    return pl.pallas_call(
        _final_kernel,
        grid=(s // tm,),
        in_specs=[pl.BlockSpec((tm, d), lambda i: (i, 0)),
                  pl.BlockSpec((d, d), lambda i: (0, 0), pipeline_mode=pl.Buffered(1)),
                  pl.BlockSpec((tm, d), lambda i: (i, 0)),
                  pl.BlockSpec((1, d), lambda i: (0, 0))],
        out_specs=pl.BlockSpec((tm, d), lambda i: (i, 0)),
        out_shape=jax.ShapeDtypeStruct((s, d), F32),
        compiler_params=_params("parallel"),
        name="wo_norm_residual",
    )(m, wo, x, gain.reshape(1, d))


def _rotary_tables(s):
    pos = jnp.arange(s, dtype=F32)
    inv_freq = ROPE_THETA ** (-jnp.arange(0, ROT_DIM, 2, dtype=F32) / ROT_DIM)
    ang = pos[:, None] * inv_freq[None, :]
    return jnp.concatenate([jnp.cos(ang), jnp.sin(ang), jnp.zeros((s, HEAD_DIM - ROT_DIM), F32)], axis=1)


def _layer(x, norm_pre, w_in, b_merge, sink, conv_w, conv_b, wa, wb, wo, norm_post):
    s, d = x.shape
    n_q = d // HEAD_DIM
    n_kv = max(n_q // GROUP, 1)
    attn_w, kv_w = n_q * HEAD_DIM, n_kv * HEAD_DIM
    qkvg_cols = 2 * attn_w + 2 * kv_w
    rest_cols = w_in.shape[1] - qkvg_cols
    merge_a_col, merge_b_col = 4 * d, 5 * d

    h = _rmsnorm(x, norm_pre)
    assert kv_w <= PROJ_COLS and qkvg_cols % kv_w == 0, (kv_w, qkvg_cols)
    qkvg, wa16, wb16, wo16 = _in_proj(h, w_in, 0, qkvg_cols, kv_w, True, [wa, wb, wo], None,
                                      "in_proj_qkvg")
    tn_b, plan = _plan_rest(rest_cols, s // _tile(s, PROJ_ROWS, 8), n_kv, s // BLOCK)
    p_rest, za = _in_proj(h, w_in, qkvg_cols, rest_cols, tn_b, False, [],
                          (plan, qkvg, _rotary_tables(s), sink, n_q, n_kv), "in_proj_rest_attn")
    zb = _short_conv(p_rest, conv_w, conv_b, d, 0)
    m = _out_merge(za, zb, wa16, wb16, p_rest, b_merge.reshape(1, 2 * d), merge_a_col, merge_b_col)
    return _final(m, wo16, x, norm_post)


@jax.jit
def kernel(x, norm_pre, w_in, b_merge, attn_sink, conv_w, conv_b, w_attn_out, w_conv_out, w_out, norm_post):
    b, s, d = x.shape
    depth = norm_pre.shape[0]
    outs = []
    for bi in range(b):
        xb = x.reshape(s, d) if b == 1 else x[bi]
        for l in range(depth):
            xb = _layer(xb, norm_pre[l], w_in[l], b_merge[l], attn_sink[l], conv_w[l], conv_b[l],
                        w_attn_out[l], w_conv_out[l], w_out[l], norm_post[l])
        outs.append(xb)
    return outs[0].reshape(1, s, d) if b == 1 else jnp.stack(outs, axis=0)
```

```python
import functools
import math
from typing import NamedTuple, Optional

import jax
import jax.numpy as jnp
from jax import lax
from jax.experimental import pallas as pl
from jax.experimental.pallas import tpu as pltpu

HEAD_DIM = 128
GROUP = 4
WINDOW = 128
BLOCK = 128
ROPE_THETA = 500000.0
ROT_DIM = HEAD_DIM // 4
ROT_HALF = ROT_DIM // 2
RMS_EPS = 1e-6
LOG2E = 1.4426950408889634
LANES = 128
BF16_SUBLANES = 16
V7X_VMEM_BYTES = 64 * 1024 * 1024
VMEM_LIMIT_BYTES = V7X_VMEM_BYTES - 6 * 1024 * 1024

PROJ_ROWS, PROJ_COLS = 1024, 1024
MERGE_ROWS, MERGE_COLS = 1024, 512
FINAL_ROWS = 256
NORM_ROWS = 512
CONV_ROWS, CONV_COLS = 1024, 1024

F32 = jnp.float32
BF16 = jnp.bfloat16


def _tile(dim, pref, unit):
    t = min(pref, dim)
    t -= t % unit
    while t > unit and dim % t:
        t -= unit
    assert t >= unit and dim % t == 0, (dim, pref, unit)
    return t


def _params(*sem):
    return pltpu.CompilerParams(dimension_semantics=sem, vmem_limit_bytes=VMEM_LIMIT_BYTES)


def _rmsnorm_kernel(x_ref, g_ref, o_ref):
    x = x_ref[...]
    ms = jnp.mean(x * x, axis=-1, keepdims=True)
    o_ref[...] = (x * lax.rsqrt(ms + RMS_EPS) * g_ref[...]).astype(o_ref.dtype)


def _rmsnorm(x, gain):
    s, d = x.shape
    tm = _tile(s, NORM_ROWS, 8)
    return pl.pallas_call(
        _rmsnorm_kernel,
        grid=(s // tm,),
        in_specs=[pl.BlockSpec((tm, d), lambda i: (i, 0)),
                  pl.BlockSpec((1, d), lambda i: (0, 0))],
        out_specs=pl.BlockSpec((tm, d), lambda i: (i, 0)),
        out_shape=jax.ShapeDtypeStruct((s, d), BF16),
        compiler_params=_params("parallel"),
        name="rmsnorm_pre",
    )(x, gain.reshape(1, d))


class _AttnPlan(NamedTuple):
    nq: int
    units_per_head: int
    units: int


def _rotary_lanes(p):
    lane = lax.broadcasted_iota(jnp.int32, p.shape, 1)
    first, second = lane < ROT_HALF, (lane >= ROT_HALF) & (lane < ROT_DIM)
    c = jnp.where(first, p, jnp.where(second, pltpu.roll(p, ROT_HALF, 1), 1.0))
    s1 = jnp.where(first, -pltpu.roll(p, HEAD_DIM - ROT_HALF, 1), 0.0)
    s2 = jnp.where(second, p, 0.0)
    return c, s1, s2


def _rotate(t, tabs):
    c, s1, s2 = tabs
    return t * c + pltpu.roll(t, HEAD_DIM - ROT_HALF, 1) * s1 + pltpu.roll(t, ROT_HALF, 1) * s2


def _key_rows(win, t, nt):
    first = jnp.concatenate([win[-BLOCK:], win[:-BLOCK]], axis=0)
    last = jnp.concatenate([win[BLOCK:], win[:BLOCK]], axis=0)
    return jnp.where(t == 0, first, jnp.where(t == nt - 1, last, win))


def _attn_unit(nq, hkv, t, nt, sink_ref, q_ref, kv_ref, gate_ref, tab_ref, o_ref, s_scr, p_scr):
    rows = GROUP * BLOCK
    scale = HEAD_DIM ** -0.5
    kfull = _key_rows(kv_ref[0, 0], t, nt)
    tabs = _rotary_lanes(tab_ref[...])
    tab_m = [x[BLOCK:(nq + 1) * BLOCK] for x in tabs]
    kfull = _rotate(kfull.astype(F32), tabs).astype(kfull.dtype)
    vfull = _key_rows(kv_ref[1, 0], t, nt)
    vext = jnp.concatenate([vfull, jnp.ones_like(vfull)], axis=1)

    for b in range(nq):
        q4 = q_ref[:, b * BLOCK:(b + 1) * BLOCK, :].reshape(rows, HEAD_DIM)
        tab_q = [jnp.concatenate([x[b * BLOCK:(b + 1) * BLOCK]] * GROUP, axis=0) for x in tab_m]
        q4 = _rotate(q4.astype(F32), tab_q).astype(q4.dtype)
        s_scr[b] = lax.dot_general(q4, kfull[b * BLOCK:(b + 3) * BLOCK], (((1,), (1,)), ((), ())),
                                   preferred_element_type=F32)

    qi = lax.broadcasted_iota(jnp.int32, (rows, BLOCK), 0) % BLOCK
    kc = lax.broadcasted_iota(jnp.int32, (rows, BLOCK), 1)
    no_prev = jnp.where(t > 0, 0, BLOCK)
    no_next = jnp.where(t < nt - 1, 0, BLOCK)
    sink_col = jnp.concatenate(
        [jnp.full((BLOCK, 1), sink_ref[hkv * GROUP + g], F32) for g in range(GROUP)], axis=0)
    sink_terms = []
    for b in range(nq):
        ok_prev = kc >= (qi + no_prev if b == 0 else qi)
        ok_next = kc <= (qi - no_next if b == nq - 1 else qi)
        s0 = jnp.where(ok_prev, s_scr[b, :, 0:BLOCK], -jnp.inf)
        s1 = s_scr[b, :, BLOCK:2 * BLOCK]
        s2 = jnp.where(ok_next, s_scr[b, :, 2 * BLOCK:3 * BLOCK], -jnp.inf)
        m_raw = jnp.max(jnp.maximum(jnp.maximum(s0, s1), s2), axis=-1, keepdims=True)
        m = jnp.maximum(m_raw * scale, sink_col)
        m2 = m * LOG2E
        for k, sk in enumerate((s0, s1, s2)):
            p_scr[b, :, k * BLOCK:(k + 1) * BLOCK] = jnp.exp2(sk * (scale * LOG2E) - m2).astype(p_scr.dtype)
        sink_terms.append(jnp.exp(sink_col - m))

    for b in range(nq):
        o2 = jnp.dot(p_scr[b], vext[b * BLOCK:(b + 3) * BLOCK], preferred_element_type=F32)
        num = o2[:, :HEAD_DIM]
        den = o2[:, HEAD_DIM:] + sink_terms[b]
        for g in range(GROUP):
            gate = gate_ref[g, b * BLOCK:(b + 1) * BLOCK, :].astype(F32)
            r = slice(g * BLOCK, (g + 1) * BLOCK)
            o_ref[b * BLOCK:(b + 1) * BLOCK, g * HEAD_DIM:(g + 1) * HEAD_DIM] = (
                (num[r] * gate) / (den[r] * (1.0 + jnp.exp(-gate)))).astype(o_ref.dtype)


def _attn_specs(plan, n_i, n_q, n_kv, nb):
    nq, upk, units = plan
    tq = nq * BLOCK
    assert (n_q + 2 * n_kv) % GROUP == 0, "attn_gate heads must start on a GROUP boundary"
    g0 = (n_q + 2 * n_kv) // GROUP
    kv_rows = tq + 2 * BLOCK
    assert nb * BLOCK >= kv_rows, (nb, nq)

    def unit(j, i):
        u = jnp.minimum(j * n_i + i, units - 1)
        return lax.div(u, upk), lax.rem(u, upk)

    def at(fn):
        return lambda j, i: fn(*unit(j, i))

    in_specs = [pl.BlockSpec(memory_space=pltpu.SMEM),
                pl.BlockSpec((GROUP, tq, HEAD_DIM), at(lambda h, t: (h, t, 0))),
                pl.BlockSpec((pl.Element(2), pl.Element(1), pl.Element(kv_rows), pl.Element(HEAD_DIM)),
                             at(lambda h, t: (GROUP, h, pl.multiple_of(
                                 jnp.clip(t * tq - BLOCK, 0, nb * BLOCK - kv_rows), BLOCK), 0))),
                pl.BlockSpec((GROUP, tq, HEAD_DIM), at(lambda h, t: (g0 + h, t, 0))),
                pl.BlockSpec((pl.Element(tq + 2 * BLOCK), pl.Element(HEAD_DIM)),
                             at(lambda h, t: (t * tq, 0)))]
    out_spec = pl.BlockSpec((tq, GROUP * HEAD_DIM), at(lambda h, t: (t, h)))
    scratch = [pltpu.VMEM((nq, GROUP * BLOCK, 3 * BLOCK), F32),
               pltpu.VMEM((nq, GROUP * BLOCK, 3 * BLOCK), BF16)]
    return in_specs, out_spec, scratch


class _ProjCfg(NamedTuple):
    head_major: bool
    n_riders: int
    kc: int
    col0: int
    attn: Optional[_AttnPlan]


N_ATTN_IN = 5


def _in_proj_kernel(cfg, h_ref, w_hbm, *refs):
    refs = list(refs)
    take = lambda n: [refs.pop(0) for _ in range(n)]
    riders_in = take(cfg.n_riders)
    attn_in = take(N_ATTN_IN if cfg.attn else 0)
    o_ref, = take(1)
    riders_out = take(cfg.n_riders)
    z_ref = take(1 if cfg.attn else 0)
    w_even, w_odd, stage, sem = take(4)
    attn_scratch = refs
    kc = cfg.kc
    j, i = pl.program_id(0), pl.program_id(1)
    n_j, n_i = pl.num_programs(0), pl.num_programs(1)
    tn = w_even.shape[1]
    step = j * n_i + i
    slot = step % 2

    def chunk_copy(jb, ic, to_slot):
        return pltpu.make_async_copy(
            w_hbm.at[pl.ds(ic * kc, kc), pl.ds(cfg.col0 + jb * tn, tn)],
            stage.at[to_slot], sem.at[to_slot])

    j_next = jnp.minimum(j + 1, n_j - 1)

    @pl.when(step == 0)
    def _():
        n_chunks = w_even.shape[0] // kc
        for ic in range(min(2, n_chunks)):
            chunk_copy(0, ic, ic).start()
        for ic in range(n_chunks):
            chunk_copy(0, ic, ic % 2).wait()
            w_even[ic * kc:(ic + 1) * kc, :] = stage[ic % 2].astype(BF16)
            if ic + 2 < n_chunks:
                chunk_copy(0, ic + 2, ic % 2).start()
        chunk_copy(j_next, 0, 0).start()

    @pl.when(step + 1 < n_j * n_i)
    def _():
        wrap = i + 1 == n_i
        chunk_copy(jnp.minimum(jnp.where(wrap, j + 2, j + 1), n_j - 1),
                   jnp.where(wrap, 0, i + 1), 1 - slot).start()

    chunk_copy(j_next, i, slot).wait()

    def compute(w_cur, w_next):
        for src, dst in zip(riders_in, riders_out):
            dst[...] = src[...].astype(dst.dtype)
        acc = jnp.dot(h_ref[...], w_cur[...], preferred_element_type=F32)
        if cfg.head_major:
            for g in range(o_ref.shape[0]):
                o_ref[g] = acc[:, g * HEAD_DIM:(g + 1) * HEAD_DIM].astype(o_ref.dtype)
        else:
            o_ref[...] = acc.astype(o_ref.dtype)
        w_next[pl.ds(pl.multiple_of(i * kc, kc), kc), :] = stage[slot].astype(BF16)
        if cfg.attn:
            unit = jnp.minimum(step, cfg.attn.units - 1)
            upk = cfg.attn.units_per_head
            _attn_unit(cfg.attn.nq, lax.div(unit, upk), lax.rem(unit, upk), upk,
                       *attn_in, *z_ref, *attn_scratch)

    pl.when(j % 2 == 0)(lambda: compute(w_even, w_odd))
    pl.when(j % 2 == 1)(lambda: compute(w_odd, w_even))


def _plan_rest(n_cols, n_i, n_kv, nb):
    best = None
    for tn in range(PROJ_COLS, 0, -LANES):
        if n_cols % tn:
            continue
        steps = (n_cols // tn) * n_i
        for nq in (1, 2, 4, 8):
            if nb % nq or n_kv * (nb // nq) > steps:
                continue
            waste = steps - n_kv * (nb // nq)
            if best is None or waste < best[0]:
                best = (waste, tn, _AttnPlan(nq, nb // nq, n_kv * (nb // nq)))
            break
    assert best is not None, (n_cols, n_i, n_kv, nb)
    return best[1], best[2]


def _in_proj(h, w_in, col0, n_cols, tn, head_major, riders, attn, name):
    s, d = h.shape
    tm = _tile(s, PROJ_ROWS, 8)
    n_j, n_i = n_cols // tn, s // tm
    kc = d // n_i
    assert n_cols % tn == 0 and d % n_i == 0 and kc % BF16_SUBLANES == 0, (n_cols, tn, d, n_i)

    in_specs = [pl.BlockSpec((tm, d), lambda j, i: (i, 0)),
                pl.BlockSpec(memory_space=pl.ANY)]
    args = [h, w_in]
    if head_major:
        assert tn % HEAD_DIM == 0
        out_specs = [pl.BlockSpec((tn // HEAD_DIM, tm, HEAD_DIM), lambda j, i: (j, i, 0))]
        out_shape = [jax.ShapeDtypeStruct((n_cols // HEAD_DIM, s, HEAD_DIM), BF16)]
    else:
        out_specs = [pl.BlockSpec((tm, tn), lambda j, i: (i, j))]
        out_shape = [jax.ShapeDtypeStruct((s, n_cols), BF16)]

    for r in riders:
        rows, cols = r.shape
        rr = next(c for c in range(BF16_SUBLANES, rows + 1, BF16_SUBLANES)
                  if rows % c == 0 and rows // c <= n_j * n_i)
        spec = pl.BlockSpec((rr, cols), functools.partial(
            lambda j, i, last: (jnp.minimum(j * n_i + i, last), 0), last=rows // rr - 1))
        in_specs.append(spec)
        args.append(r)
        out_specs.append(spec)
        out_shape.append(jax.ShapeDtypeStruct(r.shape, BF16))

    scratch = [pltpu.VMEM((d, tn), BF16), pltpu.VMEM((d, tn), BF16),
               pltpu.VMEM((2, kc, tn), F32), pltpu.SemaphoreType.DMA((2,))]
    plan = None
    if attn is not None:
        plan, qkvg, tabs, sink, n_q, n_kv = attn
        a_in, a_out, a_scratch = _attn_specs(plan, n_i, n_q, n_kv, s // BLOCK)
        in_specs += a_in
        kv4 = qkvg.reshape(-1, n_kv, s, HEAD_DIM)
        args += [sink, qkvg, kv4, qkvg, tabs]
        out_specs.append(a_out)
        out_shape.append(jax.ShapeDtypeStruct((s, n_q * HEAD_DIM), BF16))
        scratch += a_scratch

    cfg = _ProjCfg(head_major, len(riders), kc, col0, plan)
    return pl.pallas_call(
        functools.partial(_in_proj_kernel, cfg),
        grid=(n_j, n_i),
        in_specs=in_specs,
        out_specs=out_specs,
        out_shape=out_shape,
        scratch_shapes=scratch,
        compiler_params=_params("arbitrary", "arbitrary"),
        name=name,
    )(*args)


def _conv_block(i, ni, bg_ref, cg_ref, cx_ref, gt_ref, cgp_ref, cxp_ref, cgn_ref, cxn_ref, w_ref, b_ref):
    tm = bg_ref.shape[0]
    halo = cgp_ref.shape[0]
    u = cg_ref[...].astype(F32) * cx_ref[...].astype(F32)
    u_prev = cgp_ref[halo - 1:halo, :].astype(F32) * cxp_ref[halo - 1:halo, :].astype(F32)
    u_next = cgn_ref[0:1, :].astype(F32) * cxn_ref[0:1, :].astype(F32)
    u_prev = jnp.where(i > 0, u_prev, 0.0)
    u_next = jnp.where(i < ni - 1, u_next, 0.0)
    row = lax.broadcasted_iota(jnp.int32, u.shape, 0)
    up = jnp.where(row == 0, u_prev, pltpu.roll(u, 1, 0))
    dn = jnp.where(row == tm - 1, u_next, pltpu.roll(u, tm - 1, 0))
    c = up * w_ref[0:1, :] + u * w_ref[1:2, :] + dn * w_ref[2:3, :] + b_ref[...]
    gate = gt_ref[...].astype(F32)
    return bg_ref[...].astype(F32) * c * (gate * jax.nn.sigmoid(gate))


def _conv_kernel(*refs):
    o_ref = refs[-1]
    o_ref[...] = _conv_block(pl.program_id(0), pl.num_programs(0), *refs[:-1]).astype(o_ref.dtype)


def _short_conv(p_rest, conv_w, conv_b, d, col0):
    s = p_rest.shape[0]
    halo = BF16_SUBLANES
    tm = _tile(s, CONV_ROWS, halo)
    tc = _tile(math.gcd(d, col0), CONV_COLS, LANES)
    nc = d // tc
    hb = tm // halo
    last_hb = s // halo - 1

    def cur(k):
        return pl.BlockSpec((tm, tc), lambda i, c: (i, (col0 // tc) + k * nc + c))

    def prev(k):
        return pl.BlockSpec((halo, tc),
                            lambda i, c: (jnp.maximum(i * hb - 1, 0), (col0 // tc) + k * nc + c))

    def nxt(k):
        return pl.BlockSpec((halo, tc),
                            lambda i, c: (jnp.minimum((i + 1) * hb, last_hb), (col0 // tc) + k * nc + c))

    return pl.pallas_call(
        _conv_kernel,
        grid=(s // tm, nc),
        in_specs=[cur(0), cur(1), cur(2), cur(3), prev(1), prev(2), nxt(1), nxt(2),
                  pl.BlockSpec((3, tc), lambda i, c: (0, c)),
                  pl.BlockSpec((1, tc), lambda i, c: (0, c))],
        out_specs=pl.BlockSpec((tm, tc), lambda i, c: (i, c)),
        out_shape=jax.ShapeDtypeStruct((s, d), BF16),
        compiler_params=_params("parallel", "parallel"),
        name="short_conv",
    )(p_rest, p_rest, p_rest, p_rest, p_rest, p_rest, p_rest, p_rest, conv_w, conv_b.reshape(1, d))


def _merge_kernel(za_ref, zb_ref, wa_ref, wb_ref, la_ref, lb_ref, ba_ref, bb_ref, o_ref):
    ya = jnp.dot(za_ref[...], wa_ref[...], preferred_element_type=F32)
    yb = jnp.dot(zb_ref[...], wb_ref[...], preferred_element_type=F32)
    ga = jax.nn.sigmoid(la_ref[...].astype(F32) + ba_ref[...])
    gb = jax.nn.sigmoid(lb_ref[...].astype(F32) + bb_ref[...])
    o_ref[...] = (ga * ya + gb * yb).astype(o_ref.dtype)


def _out_merge(za, zb, wa, wb, p_rest, b_merge, col_a, col_b):
    s, d = za.shape
    tm = _tile(s, MERGE_ROWS, 8)
    tn = _tile(math.gcd(d, col_a), MERGE_COLS, LANES)
    nn = d // tn
    return pl.pallas_call(
        _merge_kernel,
        grid=(s // tm, nn),
        in_specs=[pl.BlockSpec((tm, d), lambda i, j: (i, 0)),
                  pl.BlockSpec((tm, d), lambda i, j: (i, 0)),
                  pl.BlockSpec((d, tn), lambda i, j: (0, j)),
                  pl.BlockSpec((d, tn), lambda i, j: (0, j)),
                  pl.BlockSpec((tm, tn), lambda i, j: (i, col_a // tn + j)),
                  pl.BlockSpec((tm, tn), lambda i, j: (i, col_b // tn + j)),
                  pl.BlockSpec((1, tn), lambda i, j: (0, j)),
                  pl.BlockSpec((1, tn), lambda i, j: (0, nn + j))],
        out_specs=pl.BlockSpec((tm, tn), lambda i, j: (i, j)),
        out_shape=jax.ShapeDtypeStruct((s, d), BF16),
        compiler_params=_params("parallel", "parallel"),
        name="out_proj_merge",
    )(za, zb, wa, wb, p_rest, p_rest, b_merge, b_merge)


def _final_kernel(m_ref, w_ref, x_ref, g_ref, o_ref):
    o = jnp.dot(m_ref[...], w_ref[...], preferred_element_type=F32)
    ms = jnp.mean(o * o, axis=-1, keepdims=True)
    o_ref[...] = x_ref[...] + o * lax.rsqrt(ms + RMS_EPS) * g_ref[...]


def _final(m, wo, x, gain):
    s, d = m.shape
    tm = _tile(s, FINAL_ROWS, 8)
    return pl.pallas_call(
        _final_kernel,
        grid=(s // tm,),
        in_specs=[pl.BlockSpec((tm, d), lambda i: (i, 0)),
                  pl.BlockSpec((d, d), lambda i: (0, 0), pipeline_mode=pl.Buffered(1)),
                  pl.BlockSpec((tm, d), lambda i: (i, 0)),
                  pl.BlockSpec((1, d), lambda i: (0, 0))],
        out_specs=pl.BlockSpec((tm, d), lambda i: (i, 0)),
        out_shape=jax.ShapeDtypeStruct((s, d), F32),
        compiler_params=_params("parallel"),
        name="wo_norm_residual",
    )(m, wo, x, gain.reshape(1, d))


def _rotary_tables(s):
    pos = jnp.arange(s, dtype=F32)
    inv_freq = ROPE_THETA ** (-jnp.arange(0, ROT_DIM, 2, dtype=F32) / ROT_DIM)
    ang = pos[:, None] * inv_freq[None, :]
    tab = jnp.concatenate([jnp.cos(ang), jnp.sin(ang), jnp.zeros((s, HEAD_DIM - ROT_DIM), F32)], axis=1)
    return jnp.pad(tab, ((BLOCK, BLOCK), (0, 0)))


def _layer(x, norm_pre, w_in, b_merge, sink, conv_w, conv_b, wa, wb, wo, norm_post):
    s, d = x.shape
    n_q = d // HEAD_DIM
    n_kv = max(n_q // GROUP, 1)
    attn_w, kv_w = n_q * HEAD_DIM, n_kv * HEAD_DIM
    qkvg_cols = 2 * attn_w + 2 * kv_w
    rest_cols = w_in.shape[1] - qkvg_cols
    merge_a_col, merge_b_col = 4 * d, 5 * d

    h = _rmsnorm(x, norm_pre)
    assert kv_w <= PROJ_COLS and qkvg_cols % kv_w == 0, (kv_w, qkvg_cols)
    qkvg, wa16, wb16, wo16 = _in_proj(h, w_in, 0, qkvg_cols, kv_w, True, [wa, wb, wo], None,
                                      "in_proj_qkvg")
    tn_b, plan = _plan_rest(rest_cols, s // _tile(s, PROJ_ROWS, 8), n_kv, s // BLOCK)
    p_rest, za = _in_proj(h, w_in, qkvg_cols, rest_cols, tn_b, False, [],
                          (plan, qkvg, _rotary_tables(s), sink, n_q, n_kv), "in_proj_rest_attn")
    zb = _short_conv(p_rest, conv_w, conv_b, d, 0)
    m = _out_merge(za, zb, wa16, wb16, p_rest, b_merge.reshape(1, 2 * d), merge_a_col, merge_b_col)
    return _final(m, wo16, x, norm_post)


@jax.jit
def kernel(x, norm_pre, w_in, b_merge, attn_sink, conv_w, conv_b, w_attn_out, w_conv_out, w_out, norm_post):
    b, s, d = x.shape
    depth = norm_pre.shape[0]
    outs = []
    for bi in range(b):
        xb = x.reshape(s, d) if b == 1 else x[bi]
        for l in range(depth):
            xb = _layer(xb, norm_pre[l], w_in[l], b_merge[l], attn_sink[l], conv_w[l], conv_b[l],
                        w_attn_out[l], w_conv_out[l], w_out[l], norm_post[l])
        outs.append(xb)
    return outs[0].reshape(1, s, d) if b == 1 else jnp.stack(outs, axis=0)
```

```python
import functools
import math
from typing import NamedTuple, Optional

import jax
import jax.numpy as jnp
from jax import lax
from jax.experimental import pallas as pl
from jax.experimental.pallas import tpu as pltpu

HEAD_DIM = 128
GROUP = 4
WINDOW = 128
BLOCK = 128
ROPE_THETA = 500000.0
ROT_DIM = HEAD_DIM // 4
ROT_HALF = ROT_DIM // 2
RMS_EPS = 1e-6
LOG2E = 1.4426950408889634
LANES = 128
BF16_SUBLANES = 16
V7X_VMEM_BYTES = 64 * 1024 * 1024
VMEM_LIMIT_BYTES = V7X_VMEM_BYTES - 6 * 1024 * 1024

PROJ_ROWS, PROJ_COLS = 1024, 1024
MERGE_ROWS, MERGE_COLS = 1024, 512
FINAL_ROWS = 256
NORM_ROWS = 512
CONV_ROWS, CONV_COLS = 1024, 1024

F32 = jnp.float32
BF16 = jnp.bfloat16


def _tile(dim, pref, unit):
    t = min(pref, dim)
    t -= t % unit
    while t > unit and dim % t:
        t -= unit
    assert t >= unit and dim % t == 0, (dim, pref, unit)
    return t


def _params(*sem):
    return pltpu.CompilerParams(dimension_semantics=sem, vmem_limit_bytes=VMEM_LIMIT_BYTES)


def _rmsnorm_kernel(x_ref, g_ref, o_ref):
    x = x_ref[...]
    ms = jnp.mean(x * x, axis=-1, keepdims=True)
    o_ref[...] = (x * lax.rsqrt(ms + RMS_EPS) * g_ref[...]).astype(o_ref.dtype)


def _rmsnorm(x, gain):
    s, d = x.shape
    tm = _tile(s, NORM_ROWS, 8)
    return pl.pallas_call(
        _rmsnorm_kernel,
        grid=(s // tm,),
        in_specs=[pl.BlockSpec((tm, d), lambda i: (i, 0)),
                  pl.BlockSpec((1, d), lambda i: (0, 0))],
        out_specs=pl.BlockSpec((tm, d), lambda i: (i, 0)),
        out_shape=jax.ShapeDtypeStruct((s, d), BF16),
        compiler_params=_params("parallel"),
        name="rmsnorm_pre",
    )(x, gain.reshape(1, d))


class _AttnPlan(NamedTuple):
    nq: int
    units_per_head: int
    units: int


def _rotary_lanes(p):
    lane = lax.broadcasted_iota(jnp.int32, p.shape, 1)
    first, second = lane < ROT_HALF, (lane >= ROT_HALF) & (lane < ROT_DIM)
    c = jnp.where(first, p, jnp.where(second, pltpu.roll(p, ROT_HALF, 1), 1.0))
    s1 = jnp.where(first, -pltpu.roll(p, HEAD_DIM - ROT_HALF, 1), 0.0)
    s2 = jnp.where(second, p, 0.0)
    return c, s1, s2


def _rotate(t, tabs):
    c, s1, s2 = tabs
    return t * c + pltpu.roll(t, HEAD_DIM - ROT_HALF, 1) * s1 + pltpu.roll(t, ROT_HALF, 1) * s2


def _attn_unit(nq, hkv, t, nt, sink_ref, q_ref, kvp_ref, kvm_ref, kvn_ref,
               gate_ref, tab_ref, o_ref, s_scr, p_scr):
    rows = GROUP * BLOCK
    scale = HEAD_DIM ** -0.5
    kfull = jnp.concatenate([kvp_ref[0, 0], kvm_ref[0, 0], kvn_ref[0, 0]], axis=0)
    tabs = _rotary_lanes(tab_ref[...])
    tab_m = [x[BLOCK:(nq + 1) * BLOCK] for x in tabs]
    kfull = _rotate(kfull.astype(F32), tabs).astype(kfull.dtype)
    vfull = jnp.concatenate([kvp_ref[1, 0], kvm_ref[1, 0], kvn_ref[1, 0]], axis=0)
    vext = jnp.concatenate([vfull, jnp.ones_like(vfull)], axis=1)

    for b in range(nq):
        q4 = q_ref[:, b * BLOCK:(b + 1) * BLOCK, :].reshape(rows, HEAD_DIM)
        tab_q = [jnp.concatenate([x[b * BLOCK:(b + 1) * BLOCK]] * GROUP, axis=0) for x in tab_m]
        q4 = _rotate(q4.astype(F32), tab_q).astype(q4.dtype)
        s_scr[b] = lax.dot_general(q4, kfull[b * BLOCK:(b + 3) * BLOCK], (((1,), (1,)), ((), ())),
                                   preferred_element_type=F32)

    qi = lax.broadcasted_iota(jnp.int32, (rows, BLOCK), 0) % BLOCK
    kc = lax.broadcasted_iota(jnp.int32, (rows, BLOCK), 1)
    no_prev = jnp.where(t > 0, 0, BLOCK)
    no_next = jnp.where(t < nt - 1, 0, BLOCK)
    sink_col = jnp.concatenate(
        [jnp.full((BLOCK, 1), sink_ref[hkv * GROUP + g], F32) for g in range(GROUP)], axis=0)
    sink_terms = []
    for b in range(nq):
        ok_prev = kc >= (qi + no_prev if b == 0 else qi)
        ok_next = kc <= (qi - no_next if b == nq - 1 else qi)
        s0 = jnp.where(ok_prev, s_scr[b, :, 0:BLOCK], -jnp.inf)
        s1 = s_scr[b, :, BLOCK:2 * BLOCK]
        s2 = jnp.where(ok_next, s_scr[b, :, 2 * BLOCK:3 * BLOCK], -jnp.inf)
        m_raw = jnp.max(jnp.maximum(jnp.maximum(s0, s1), s2), axis=-1, keepdims=True)
        m = jnp.maximum(m_raw * scale, sink_col)
        m2 = m * LOG2E
        for k, sk in enumerate((s0, s1, s2)):
            p_scr[b, :, k * BLOCK:(k + 1) * BLOCK] = jnp.exp2(sk * (scale * LOG2E) - m2).astype(p_scr.dtype)
        sink_terms.append(jnp.exp(sink_col - m))

    for b in range(nq):
        o2 = jnp.dot(p_scr[b], vext[b * BLOCK:(b + 3) * BLOCK], preferred_element_type=F32)
        num = o2[:, :HEAD_DIM]
        den = o2[:, HEAD_DIM:] + sink_terms[b]
        for g in range(GROUP):
            gate = gate_ref[g, b * BLOCK:(b + 1) * BLOCK, :].astype(F32)
            r = slice(g * BLOCK, (g + 1) * BLOCK)
            o_ref[b * BLOCK:(b + 1) * BLOCK, g * HEAD_DIM:(g + 1) * HEAD_DIM] = (
                (num[r] * gate) / (den[r] * (1.0 + jnp.exp(-gate)))).astype(o_ref.dtype)


def _attn_specs(plan, n_i, n_q, n_kv, nb):
    nq, upk, units = plan
    tq = nq * BLOCK
    assert (n_q + 2 * n_kv) % GROUP == 0, "attn_gate heads must start on a GROUP boundary"
    g0 = (n_q + 2 * n_kv) // GROUP
    kv_pair = GROUP // 2

    def unit(j, i):
        u = jnp.minimum(j * n_i + i, units - 1)
        return lax.div(u, upk), lax.rem(u, upk)

    def at(fn):
        return lambda j, i: fn(*unit(j, i))

    in_specs = [pl.BlockSpec(memory_space=pltpu.SMEM),
                pl.BlockSpec((GROUP, tq, HEAD_DIM), at(lambda h, t: (h, t, 0))),
                pl.BlockSpec((2, 1, BLOCK, HEAD_DIM),
                             at(lambda h, t: (kv_pair, h, jnp.maximum(t * nq - 1, 0), 0))),
                pl.BlockSpec((2, 1, tq, HEAD_DIM), at(lambda h, t: (kv_pair, h, t, 0))),
                pl.BlockSpec((2, 1, BLOCK, HEAD_DIM),
                             at(lambda h, t: (kv_pair, h, jnp.minimum((t + 1) * nq, nb - 1), 0))),
                pl.BlockSpec((GROUP, tq, HEAD_DIM), at(lambda h, t: (g0 + h, t, 0))),
                pl.BlockSpec((pl.Element(tq + 2 * BLOCK), pl.Element(HEAD_DIM)),
                             at(lambda h, t: (t * tq, 0)))]
    out_spec = pl.BlockSpec((tq, GROUP * HEAD_DIM), at(lambda h, t: (t, h)))
    scratch = [pltpu.VMEM((nq, GROUP * BLOCK, 3 * BLOCK), F32),
               pltpu.VMEM((nq, GROUP * BLOCK, 3 * BLOCK), BF16)]
    return in_specs, out_spec, scratch


class _ProjCfg(NamedTuple):
    head_major: bool
    n_riders: int
    kc: int
    col0: int
    attn: Optional[_AttnPlan]


N_ATTN_IN = 7


def _in_proj_kernel(cfg, h_ref, w_hbm, *refs):
    refs = list(refs)
    take = lambda n: [refs.pop(0) for _ in range(n)]
    riders_in = take(cfg.n_riders)
    attn_in = take(N_ATTN_IN if cfg.attn else 0)
    o_ref, = take(1)
    riders_out = take(cfg.n_riders)
    z_ref = take(1 if cfg.attn else 0)
    w_even, w_odd, stage, sem = take(4)
    attn_scratch = refs
    kc = cfg.kc
    j, i = pl.program_id(0), pl.program_id(1)
    n_j, n_i = pl.num_programs(0), pl.num_programs(1)
    tn = w_even.shape[1]
    step = j * n_i + i
    slot = step % 2

    def chunk_copy(jb, ic, to_slot):
        return pltpu.make_async_copy(
            w_hbm.at[pl.ds(ic * kc, kc), pl.ds(cfg.col0 + jb * tn, tn)],
            stage.at[to_slot], sem.at[to_slot])

    j_next = jnp.minimum(j + 1, n_j - 1)

    @pl.when(step == 0)
    def _():
        n_chunks = w_even.shape[0] // kc
        for ic in range(min(2, n_chunks)):
            chunk_copy(0, ic, ic).start()
        for ic in range(n_chunks):
            chunk_copy(0, ic, ic % 2).wait()
            w_even[ic * kc:(ic + 1) * kc, :] = stage[ic % 2].astype(BF16)
            if ic + 2 < n_chunks:
                chunk_copy(0, ic + 2, ic % 2).start()
        chunk_copy(j_next, 0, 0).start()

    @pl.when(step + 1 < n_j * n_i)
    def _():
        wrap = i + 1 == n_i
        chunk_copy(jnp.minimum(jnp.where(wrap, j + 2, j + 1), n_j - 1),
                   jnp.where(wrap, 0, i + 1), 1 - slot).start()

    chunk_copy(j_next, i, slot).wait()

    def compute(w_cur, w_next):
        for src, dst in zip(riders_in, riders_out):
            dst[...] = src[...].astype(dst.dtype)
        acc = jnp.dot(h_ref[...], w_cur[...], preferred_element_type=F32)
        if cfg.head_major:
            for g in range(o_ref.shape[0]):
                o_ref[g] = acc[:, g * HEAD_DIM:(g + 1) * HEAD_DIM].astype(o_ref.dtype)
        else:
            o_ref[...] = acc.astype(o_ref.dtype)
        w_next[pl.ds(pl.multiple_of(i * kc, kc), kc), :] = stage[slot].astype(BF16)
        if cfg.attn:
            unit = jnp.minimum(step, cfg.attn.units - 1)
            upk = cfg.attn.units_per_head
            _attn_unit(cfg.attn.nq, lax.div(unit, upk), lax.rem(unit, upk), upk,
                       *attn_in, *z_ref, *attn_scratch)

    pl.when(j % 2 == 0)(lambda: compute(w_even, w_odd))
    pl.when(j % 2 == 1)(lambda: compute(w_odd, w_even))


def _plan_rest(n_cols, n_i, n_kv, nb):
    best = None
    for tn in range(PROJ_COLS, 0, -LANES):
        if n_cols % tn:
            continue
        steps = (n_cols // tn) * n_i
        for nq in (1, 2, 4, 8):
            if nb % nq or n_kv * (nb // nq) > steps:
                continue
            waste = steps - n_kv * (nb // nq)
            if best is None or waste < best[0]:
                best = (waste, tn, _AttnPlan(nq, nb // nq, n_kv * (nb // nq)))
            break
    assert best is not None, (n_cols, n_i, n_kv, nb)
    return best[1], best[2]


def _in_proj(h, w_in, col0, n_cols, tn, head_major, riders, attn, name):
    s, d = h.shape
    tm = _tile(s, PROJ_ROWS, 8)
    n_j, n_i = n_cols // tn, s // tm
    kc = d // n_i
    assert n_cols % tn == 0 and d % n_i == 0 and kc % BF16_SUBLANES == 0, (n_cols, tn, d, n_i)

    in_specs = [pl.BlockSpec((tm, d), lambda j, i: (i, 0)),
                pl.BlockSpec(memory_space=pl.ANY)]
    args = [h, w_in]
    if head_major:
        assert tn % HEAD_DIM == 0
        out_specs = [pl.BlockSpec((tn // HEAD_DIM, tm, HEAD_DIM), lambda j, i: (j, i, 0))]
        out_shape = [jax.ShapeDtypeStruct((n_cols // HEAD_DIM, s, HEAD_DIM), BF16)]
    else:
        out_specs = [pl.BlockSpec((tm, tn), lambda j, i: (i, j))]
        out_shape = [jax.ShapeDtypeStruct((s, n_cols), BF16)]

    for r in riders:
        rows, cols = r.shape
        rr = next(c for c in range(BF16_SUBLANES, rows + 1, BF16_SUBLANES)
                  if rows % c == 0 and rows // c <= n_j * n_i)
        spec = pl.BlockSpec((rr, cols), functools.partial(
            lambda j, i, last: (jnp.minimum(j * n_i + i, last), 0), last=rows // rr - 1))
        in_specs.append(spec)
        args.append(r)
        out_specs.append(spec)
        out_shape.append(jax.ShapeDtypeStruct(r.shape, BF16))

    scratch = [pltpu.VMEM((d, tn), BF16), pltpu.VMEM((d, tn), BF16),
               pltpu.VMEM((2, kc, tn), F32), pltpu.SemaphoreType.DMA((2,))]
    plan = None
    if attn is not None:
        plan, qkvg, tabs, sink, n_q, n_kv = attn
        a_in, a_out, a_scratch = _attn_specs(plan, n_i, n_q, n_kv, s // BLOCK)
        in_specs += a_in
        kv4 = qkvg.reshape(-1, n_kv, s, HEAD_DIM)
        args += [sink, qkvg, kv4, kv4, kv4, qkvg, tabs]
        out_specs.append(a_out)
        out_shape.append(jax.ShapeDtypeStruct((s, n_q * HEAD_DIM), BF16))
        scratch += a_scratch

    cfg = _ProjCfg(head_major, len(riders), kc, col0, plan)
    return pl.pallas_call(
        functools.partial(_in_proj_kernel, cfg),
        grid=(n_j, n_i),
        in_specs=in_specs,
        out_specs=out_specs,
        out_shape=out_shape,
        scratch_shapes=scratch,
        compiler_params=_params("arbitrary", "arbitrary"),
        name=name,
    )(*args)


def _conv_block(i, ni, bg_ref, cg_ref, cx_ref, gt_ref, cgp_ref, cxp_ref, cgn_ref, cxn_ref, w_ref, b_ref):
    tm = bg_ref.shape[0]
    halo = cgp_ref.shape[0]
    u = cg_ref[...].astype(F32) * cx_ref[...].astype(F32)
    u_prev = cgp_ref[halo - 1:halo, :].astype(F32) * cxp_ref[halo - 1:halo, :].astype(F32)
    u_next = cgn_ref[0:1, :].astype(F32) * cxn_ref[0:1, :].astype(F32)
    u_prev = jnp.where(i > 0, u_prev, 0.0)
    u_next = jnp.where(i < ni - 1, u_next, 0.0)
    row = lax.broadcasted_iota(jnp.int32, u.shape, 0)
    up = jnp.where(row == 0, u_prev, pltpu.roll(u, 1, 0))
    dn = jnp.where(row == tm - 1, u_next, pltpu.roll(u, tm - 1, 0))
    c = up * w_ref[0:1, :] + u * w_ref[1:2, :] + dn * w_ref[2:3, :] + b_ref[...]
    half = 0.5 * gt_ref[...].astype(F32)
    return bg_ref[...].astype(F32) * c * (half + half * jnp.tanh(half))


def _conv_kernel(*refs):
    o_ref = refs[-1]
    o_ref[...] = _conv_block(pl.program_id(0), pl.num_programs(0), *refs[:-1]).astype(o_ref.dtype)


def _short_conv(p_rest, conv_w, conv_b, d, col0):
    s = p_rest.shape[0]
    halo = BF16_SUBLANES
    tm = _tile(s, CONV_ROWS, halo)
    tc = _tile(math.gcd(d, col0), CONV_COLS, LANES)
    nc = d // tc
    hb = tm // halo
    last_hb = s // halo - 1

    def cur(k):
        return pl.BlockSpec((tm, tc), lambda i, c: (i, (col0 // tc) + k * nc + c))

    def prev(k):
        return pl.BlockSpec((halo, tc),
                            lambda i, c: (jnp.maximum(i * hb - 1, 0), (col0 // tc) + k * nc + c))

    def nxt(k):
        return pl.BlockSpec((halo, tc),
                            lambda i, c: (jnp.minimum((i + 1) * hb, last_hb), (col0 // tc) + k * nc + c))

    return pl.pallas_call(
        _conv_kernel,
        grid=(s // tm, nc),
        in_specs=[cur(0), cur(1), cur(2), cur(3), prev(1), prev(2), nxt(1), nxt(2),
                  pl.BlockSpec((3, tc), lambda i, c: (0, c)),
                  pl.BlockSpec((1, tc), lambda i, c: (0, c))],
        out_specs=pl.BlockSpec((tm, tc), lambda i, c: (i, c)),
        out_shape=jax.ShapeDtypeStruct((s, d), BF16),
        compiler_params=_params("parallel", "parallel"),
        name="short_conv",
    )(p_rest, p_rest, p_rest, p_rest, p_rest, p_rest, p_rest, p_rest, conv_w, conv_b.reshape(1, d))


def _merge_kernel(za_ref, zb_ref, wa_ref, wb_ref, la_ref, lb_ref, ba_ref, bb_ref, o_ref):
    ya = jnp.dot(za_ref[...], wa_ref[...], preferred_element_type=F32)
    yb = jnp.dot(zb_ref[...], wb_ref[...], preferred_element_type=F32)
    ga = jax.nn.sigmoid(la_ref[...].astype(F32) + ba_ref[...])
    gb = jax.nn.sigmoid(lb_ref[...].astype(F32) + bb_ref[...])
    o_ref[...] = (ga * ya + gb * yb).astype(o_ref.dtype)


def _out_merge(za, zb, wa, wb, p_rest, b_merge, col_a, col_b):
    s, d = za.shape
    tm = _tile(s, MERGE_ROWS, 8)
    tn = _tile(math.gcd(d, col_a), MERGE_COLS, LANES)
    nn = d // tn
    return pl.pallas_call(
        _merge_kernel,
        grid=(s // tm, nn),
        in_specs=[pl.BlockSpec((tm, d), lambda i, j: (i, 0)),
                  pl.BlockSpec((tm, d), lambda i, j: (i, 0)),
                  pl.BlockSpec((d, tn), lambda i, j: (0, j)),
                  pl.BlockSpec((d, tn), lambda i, j: (0, j)),
                  pl.BlockSpec((tm, tn), lambda i, j: (i, col_a // tn + j)),
                  pl.BlockSpec((tm, tn), lambda i, j: (i, col_b // tn + j)),
                  pl.BlockSpec((1, tn), lambda i, j: (0, j)),
                  pl.BlockSpec((1, tn), lambda i, j: (0, nn + j))],
        out_specs=pl.BlockSpec((tm, tn), lambda i, j: (i, j)),
        out_shape=jax.ShapeDtypeStruct((s, d), BF16),
        compiler_params=_params("parallel", "parallel"),
        name="out_proj_merge",
    )(za, zb, wa, wb, p_rest, p_rest, b_merge, b_merge)


def _final_kernel(m_ref, w_ref, x_ref, g_ref, o_ref):
    o = jnp.dot(m_ref[...], w_ref[...], preferred_element_type=F32)
    ms = jnp.mean(o * o, axis=-1, keepdims=True)
    o_ref[...] = x_ref[...] + o * lax.rsqrt(ms + RMS_EPS) * g_ref[...]


def _final(m, wo, x, gain):
    s, d = m.shape
    tm = _tile(s, FINAL_ROWS, 8)
    return pl.pallas_call(
        _final_kernel,
        grid=(s // tm,),
        in_specs=[pl.BlockSpec((tm, d), lambda i: (i, 0)),
                  pl.BlockSpec((d, d), lambda i: (0, 0), pipeline_mode=pl.Buffered(1)),
                  pl.BlockSpec((tm, d), lambda i: (i, 0)),
                  pl.BlockSpec((1, d), lambda i: (0, 0))],
        out_specs=pl.BlockSpec((tm, d), lambda i: (i, 0)),
        out_shape=jax.ShapeDtypeStruct((s, d), F32),
        compiler_params=_params("parallel"),
        name="wo_norm_residual",
    )(m, wo, x, gain.reshape(1, d))


def _rotary_tables(s):
    pos = jnp.arange(s, dtype=F32)
    inv_freq = ROPE_THETA ** (-jnp.arange(0, ROT_DIM, 2, dtype=F32) / ROT_DIM)
    ang = pos[:, None] * inv_freq[None, :]
    tab = jnp.concatenate([jnp.cos(ang), jnp.sin(ang), jnp.zeros((s, HEAD_DIM - ROT_DIM), F32)], axis=1)
    return jnp.pad(tab, ((BLOCK, BLOCK), (0, 0)))


def _layer(x, norm_pre, w_in, b_merge, sink, conv_w, conv_b, wa, wb, wo, norm_post):
    s, d = x.shape
    n_q = d // HEAD_DIM
    n_kv = max(n_q // GROUP, 1)
    attn_w, kv_w = n_q * HEAD_DIM, n_kv * HEAD_DIM
    qkvg_cols = 2 * attn_w + 2 * kv_w
    rest_cols = w_in.shape[1] - qkvg_cols
    merge_a_col, merge_b_col = 4 * d, 5 * d

    h = _rmsnorm(x, norm_pre)
    assert kv_w <= PROJ_COLS and qkvg_cols % kv_w == 0, (kv_w, qkvg_cols)
    qkvg, wa16, wb16, wo16 = _in_proj(h, w_in, 0, qkvg_cols, kv_w, True, [wa, wb, wo], None,
                                      "in_proj_qkvg")
    tn_b, plan = _plan_rest(rest_cols, s // _tile(s, PROJ_ROWS, 8), n_kv, s // BLOCK)
    p_rest, za = _in_proj(h, w_in, qkvg_cols, rest_cols, tn_b, False, [],
                          (plan, qkvg, _rotary_tables(s), sink, n_q, n_kv), "in_proj_rest_attn")
    zb = _short_conv(p_rest, conv_w, conv_b, d, 0)
    m = _out_merge(za, zb, wa16, wb16, p_rest, b_merge.reshape(1, 2 * d), merge_a_col, merge_b_col)
    return _final(m, wo16, x, norm_post)


@jax.jit
def kernel(x, norm_pre, w_in, b_merge, attn_sink, conv_w, conv_b, w_attn_out, w_conv_out, w_out, norm_post):
    b, s, d = x.shape
    depth = norm_pre.shape[0]
    outs = []
    for bi in range(b):
        xb = x.reshape(s, d) if b == 1 else x[bi]
        for l in range(depth):
            xb = _layer(xb, norm_pre[l], w_in[l], b_merge[l], attn_sink[l], conv_w[l], conv_b[l],
                        w_attn_out[l], w_conv_out[l], w_out[l], norm_post[l])
        outs.append(xb)
    return outs[0].reshape(1, s, d) if b == 1 else jnp.stack(outs, axis=0)
```

```python
import functools
import math
from typing import NamedTuple, Optional

import jax
import jax.numpy as jnp
from jax import lax
from jax.experimental import pallas as pl
from jax.experimental.pallas import tpu as pltpu

HEAD_DIM = 128
GROUP = 4
WINDOW = 128
BLOCK = 128
ROPE_THETA = 500000.0
ROT_DIM = HEAD_DIM // 4
ROT_HALF = ROT_DIM // 2
RMS_EPS = 1e-6
LOG2E = 1.4426950408889634
LANES = 128
BF16_SUBLANES = 16
V7X_VMEM_BYTES = 64 * 1024 * 1024
VMEM_LIMIT_BYTES = V7X_VMEM_BYTES - 6 * 1024 * 1024

PROJ_ROWS, PROJ_COLS = 1024, 1024
MERGE_ROWS, MERGE_COLS = 1024, 512
FINAL_ROWS = 256
NORM_ROWS = 512
CONV_ROWS, CONV_COLS = 1024, 1024

F32 = jnp.float32
BF16 = jnp.bfloat16


def _tile(dim, pref, unit):
    t = min(pref, dim)
    t -= t % unit
    while t > unit and dim % t:
        t -= unit
    assert t >= unit and dim % t == 0, (dim, pref, unit)
    return t


def _params(*sem):
    return pltpu.CompilerParams(dimension_semantics=sem, vmem_limit_bytes=VMEM_LIMIT_BYTES)


def _rmsnorm_kernel(x_ref, g_ref, o_ref):
    x = x_ref[...]
    ms = jnp.mean(x * x, axis=-1, keepdims=True)
    o_ref[...] = (x * lax.rsqrt(ms + RMS_EPS) * g_ref[...]).astype(o_ref.dtype)


def _rmsnorm(x, gain):
    s, d = x.shape
    tm = _tile(s, NORM_ROWS, 8)
    return pl.pallas_call(
        _rmsnorm_kernel,
        grid=(s // tm,),
        in_specs=[pl.BlockSpec((tm, d), lambda i: (i, 0)),
                  pl.BlockSpec((1, d), lambda i: (0, 0))],
        out_specs=pl.BlockSpec((tm, d), lambda i: (i, 0)),
        out_shape=jax.ShapeDtypeStruct((s, d), BF16),
        compiler_params=_params("parallel"),
        name="rmsnorm_pre",
    )(x, gain.reshape(1, d))


class _AttnPlan(NamedTuple):
    nq: int
    units_per_head: int
    units: int


def _rotary_lanes(p):
    lane = lax.broadcasted_iota(jnp.int32, p.shape, 1)
    first, second = lane < ROT_HALF, (lane >= ROT_HALF) & (lane < ROT_DIM)
    c = jnp.where(first, p, jnp.where(second, pltpu.roll(p, ROT_HALF, 1), 1.0))
    s1 = jnp.where(first, -pltpu.roll(p, HEAD_DIM - ROT_HALF, 1), 0.0)
    s2 = jnp.where(second, p, 0.0)
    return c, s1, s2


def _rotate(t, tabs):
    c, s1, s2 = tabs
    return t * c + pltpu.roll(t, HEAD_DIM - ROT_HALF, 1) * s1 + pltpu.roll(t, ROT_HALF, 1) * s2


def _attn_unit(nq, hkv, t, nt, sink_ref, q_ref, kvp_ref, kvm_ref, kvn_ref,
               gate_ref, tab_ref, o_ref, s_scr, p_scr):
    rows = GROUP * BLOCK
    scale = HEAD_DIM ** -0.5
    kfull = jnp.concatenate([kvp_ref[0, 0], kvm_ref[0, 0], kvn_ref[0, 0]], axis=0)
    tabs = _rotary_lanes(tab_ref[...])
    tab_m = [x[BLOCK:(nq + 1) * BLOCK] for x in tabs]
    kfull = _rotate(kfull.astype(F32), tabs).astype(kfull.dtype)
    vfull = jnp.concatenate([kvp_ref[1, 0], kvm_ref[1, 0], kvn_ref[1, 0]], axis=0)
    vext = jnp.concatenate([vfull, jnp.ones_like(vfull)], axis=1)

    for b in range(nq):
        q4 = q_ref[:, b * BLOCK:(b + 1) * BLOCK, :].reshape(rows, HEAD_DIM)
        tab_q = [jnp.concatenate([x[b * BLOCK:(b + 1) * BLOCK]] * GROUP, axis=0) for x in tab_m]
        q4 = _rotate(q4.astype(F32), tab_q).astype(q4.dtype)
        s_scr[b] = lax.dot_general(q4, kfull[b * BLOCK:(b + 3) * BLOCK], (((1,), (1,)), ((), ())),
                                   preferred_element_type=F32)

    qi = lax.broadcasted_iota(jnp.int32, (rows, BLOCK), 0) % BLOCK
    kc = lax.broadcasted_iota(jnp.int32, (rows, BLOCK), 1)
    no_prev = jnp.where(t > 0, 0, BLOCK)
    no_next = jnp.where(t < nt - 1, 0, BLOCK)
    sink_col = jnp.concatenate(
        [jnp.full((BLOCK, 1), sink_ref[hkv * GROUP + g], F32) for g in range(GROUP)], axis=0)
    sink_terms = []
    for b in range(nq):
        ok_prev = kc >= (qi + no_prev if b == 0 else qi)
        ok_next = kc <= (qi - no_next if b == nq - 1 else qi)
        s0 = jnp.where(ok_prev, s_scr[b, :, 0:BLOCK], -jnp.inf)
        s1 = s_scr[b, :, BLOCK:2 * BLOCK]
        s2 = jnp.where(ok_next, s_scr[b, :, 2 * BLOCK:3 * BLOCK], -jnp.inf)
        m_raw = jnp.max(jnp.maximum(jnp.maximum(s0, s1), s2), axis=-1, keepdims=True)
        m = jnp.maximum(m_raw * scale, sink_col)
        m2 = m * LOG2E
        for k, sk in enumerate((s0, s1, s2)):
            p_scr[b, :, k * BLOCK:(k + 1) * BLOCK] = jnp.exp2(sk * (scale * LOG2E) - m2).astype(p_scr.dtype)
        sink_terms.append(jnp.exp(sink_col - m))

    for b in range(nq):
        o2 = jnp.dot(p_scr[b], vext[b * BLOCK:(b + 3) * BLOCK], preferred_element_type=F32)
        num = o2[:, :HEAD_DIM]
        den = o2[:, HEAD_DIM:] + sink_terms[b]
        for g in range(GROUP):
            gate = gate_ref[g, b * BLOCK:(b + 1) * BLOCK, :].astype(F32)
            r = slice(g * BLOCK, (g + 1) * BLOCK)
            o_ref[b * BLOCK:(b + 1) * BLOCK, g * HEAD_DIM:(g + 1) * HEAD_DIM] = (
                (num[r] * gate) / (den[r] * (1.0 + jnp.exp(-gate)))).astype(o_ref.dtype)


def _attn_specs(plan, n_i, n_q, n_kv, nb):
    nq, upk, units = plan
    tq = nq * BLOCK
    assert (n_q + 2 * n_kv) % GROUP == 0, "attn_gate heads must start on a GROUP boundary"
    g0 = (n_q + 2 * n_kv) // GROUP
    kv_pair = GROUP // 2

    def unit(j, i):
        u = jnp.minimum(j * n_i + i, units - 1)
        return lax.div(u, upk), lax.rem(u, upk)

    def at(fn):
        return lambda j, i: fn(*unit(j, i))

    in_specs = [pl.BlockSpec(memory_space=pltpu.SMEM),
                pl.BlockSpec((GROUP, tq, HEAD_DIM), at(lambda h, t: (h, t, 0))),
                pl.BlockSpec((2, 1, BLOCK, HEAD_DIM),
                             at(lambda h, t: (kv_pair, h, jnp.maximum(t * nq - 1, 0), 0))),
                pl.BlockSpec((2, 1, tq, HEAD_DIM), at(lambda h, t: (kv_pair, h, t, 0))),
                pl.BlockSpec((2, 1, BLOCK, HEAD_DIM),
                             at(lambda h, t: (kv_pair, h, jnp.minimum((t + 1) * nq, nb - 1), 0))),
                pl.BlockSpec((GROUP, tq, HEAD_DIM), at(lambda h, t: (g0 + h, t, 0))),
                pl.BlockSpec((pl.Element(tq + 2 * BLOCK), pl.Element(HEAD_DIM)),
                             at(lambda h, t: (t * tq, 0)))]
    out_spec = pl.BlockSpec((tq, GROUP * HEAD_DIM), at(lambda h, t: (t, h)))
    scratch = [pltpu.VMEM((nq, GROUP * BLOCK, 3 * BLOCK), F32),
               pltpu.VMEM((nq, GROUP * BLOCK, 3 * BLOCK), BF16)]
    return in_specs, out_spec, scratch


class _ProjCfg(NamedTuple):
    head_major: bool
    n_riders: int
    kc: int
    col0: int
    attn: Optional[_AttnPlan]


N_ATTN_IN = 7


def _in_proj_kernel(cfg, h_ref, w_hbm, *refs):
    refs = list(refs)
    take = lambda n: [refs.pop(0) for _ in range(n)]
    riders_in = take(cfg.n_riders)
    attn_in = take(N_ATTN_IN if cfg.attn else 0)
    o_ref, = take(1)
    riders_out = take(cfg.n_riders)
    z_ref = take(1 if cfg.attn else 0)
    w_even, w_odd, stage, sem = take(4)
    attn_scratch = refs
    kc = cfg.kc
    j, i = pl.program_id(0), pl.program_id(1)
    n_j, n_i = pl.num_programs(0), pl.num_programs(1)
    tn = w_even.shape[1]
    step = j * n_i + i
    slot = step % 2

    def chunk_copy(jb, ic, to_slot):
        return pltpu.make_async_copy(
            w_hbm.at[pl.ds(ic * kc, kc), pl.ds(cfg.col0 + jb * tn, tn)],
            stage.at[to_slot], sem.at[to_slot])

    j_next = jnp.minimum(j + 1, n_j - 1)

    @pl.when(step == 0)
    def _():
        n_chunks = w_even.shape[0] // kc
        for ic in range(min(2, n_chunks)):
            chunk_copy(0, ic, ic).start()
        for ic in range(n_chunks):
            chunk_copy(0, ic, ic % 2).wait()
            w_even[ic * kc:(ic + 1) * kc, :] = stage[ic % 2].astype(BF16)
            if ic + 2 < n_chunks:
                chunk_copy(0, ic + 2, ic % 2).start()
        chunk_copy(j_next, 0, 0).start()

    @pl.when(step + 1 < n_j * n_i)
    def _():
        wrap = i + 1 == n_i
        chunk_copy(jnp.minimum(jnp.where(wrap, j + 2, j + 1), n_j - 1),
                   jnp.where(wrap, 0, i + 1), 1 - slot).start()

    chunk_copy(j_next, i, slot).wait()

    def compute(w_cur, w_next):
        for src, dst in zip(riders_in, riders_out):
            dst[...] = src[...].astype(dst.dtype)
        acc = jnp.dot(h_ref[...], w_cur[...], preferred_element_type=F32)
        if cfg.head_major:
            for g in range(o_ref.shape[0]):
                o_ref[g] = acc[:, g * HEAD_DIM:(g + 1) * HEAD_DIM].astype(o_ref.dtype)
        else:
            o_ref[...] = acc.astype(o_ref.dtype)
        w_next[pl.ds(pl.multiple_of(i * kc, kc), kc), :] = stage[slot].astype(BF16)
        if cfg.attn:
            unit = jnp.minimum(step, cfg.attn.units - 1)
            upk = cfg.attn.units_per_head
            _attn_unit(cfg.attn.nq, lax.div(unit, upk), lax.rem(unit, upk), upk,
                       *attn_in, *z_ref, *attn_scratch)

    pl.when(j % 2 == 0)(lambda: compute(w_even, w_odd))
    pl.when(j % 2 == 1)(lambda: compute(w_odd, w_even))


def _plan_rest(n_cols, n_i, n_kv, nb):
    best = None
    for tn in range(PROJ_COLS, 0, -LANES):
        if n_cols % tn:
            continue
        steps = (n_cols // tn) * n_i
        for nq in (1, 2, 4, 8):
            if nb % nq or n_kv * (nb // nq) > steps:
                continue
            waste = steps - n_kv * (nb // nq)
            if best is None or waste < best[0]:
                best = (waste, tn, _AttnPlan(nq, nb // nq, n_kv * (nb // nq)))
            break
    assert best is not None, (n_cols, n_i, n_kv, nb)
    return best[1], best[2]


def _in_proj(h, w_in, col0, n_cols, tn, head_major, riders, attn, name):
    s, d = h.shape
    tm = _tile(s, PROJ_ROWS, 8)
    n_j, n_i = n_cols // tn, s // tm
    kc = d // n_i
    assert n_cols % tn == 0 and d % n_i == 0 and kc % BF16_SUBLANES == 0, (n_cols, tn, d, n_i)

    in_specs = [pl.BlockSpec((tm, d), lambda j, i: (i, 0)),
                pl.BlockSpec(memory_space=pl.ANY)]
    args = [h, w_in]
    if head_major:
        assert tn % HEAD_DIM == 0
        out_specs = [pl.BlockSpec((tn // HEAD_DIM, tm, HEAD_DIM), lambda j, i: (j, i, 0))]
        out_shape = [jax.ShapeDtypeStruct((n_cols // HEAD_DIM, s, HEAD_DIM), BF16)]
    else:
        out_specs = [pl.BlockSpec((tm, tn), lambda j, i: (i, j))]
        out_shape = [jax.ShapeDtypeStruct((s, n_cols), BF16)]

    for r in riders:
        rows, cols = r.shape
        rr = next(c for c in range(BF16_SUBLANES, rows + 1, BF16_SUBLANES)
                  if rows % c == 0 and rows // c <= n_j * n_i)
        spec = pl.BlockSpec((rr, cols), functools.partial(
            lambda j, i, last: (jnp.minimum(j * n_i + i, last), 0), last=rows // rr - 1))
        in_specs.append(spec)
        args.append(r)
        out_specs.append(spec)
        out_shape.append(jax.ShapeDtypeStruct(r.shape, BF16))

    scratch = [pltpu.VMEM((d, tn), BF16), pltpu.VMEM((d, tn), BF16),
               pltpu.VMEM((2, kc, tn), F32), pltpu.SemaphoreType.DMA((2,))]
    plan = None
    if attn is not None:
        plan, qkvg, tabs, sink, n_q, n_kv = attn
        a_in, a_out, a_scratch = _attn_specs(plan, n_i, n_q, n_kv, s // BLOCK)
        in_specs += a_in
        kv4 = qkvg.reshape(-1, n_kv, s, HEAD_DIM)
        args += [sink, qkvg, kv4, kv4, kv4, qkvg, tabs]
        out_specs.append(a_out)
        out_shape.append(jax.ShapeDtypeStruct((s, n_q * HEAD_DIM), BF16))
        scratch += a_scratch

    cfg = _ProjCfg(head_major, len(riders), kc, col0, plan)
    return pl.pallas_call(
        functools.partial(_in_proj_kernel, cfg),
        grid=(n_j, n_i),
        in_specs=in_specs,
        out_specs=out_specs,
        out_shape=out_shape,
        scratch_shapes=scratch,
        compiler_params=_params("arbitrary", "arbitrary"),
        name=name,
    )(*args)


CONV_COL_CHUNKS = 4


def _conv_block(i, ni, cols, bg_ref, cg_ref, cx_ref, gt_ref, cgp_ref, cxp_ref, cgn_ref, cxn_ref,
                w_ref, b_ref):
    tm = bg_ref.shape[0]
    halo = cgp_ref.shape[0]
    u = cg_ref[:, cols].astype(F32) * cx_ref[:, cols].astype(F32)
    u_prev = cgp_ref[halo - 1:halo, cols].astype(F32) * cxp_ref[halo - 1:halo, cols].astype(F32)
    u_next = cgn_ref[0:1, cols].astype(F32) * cxn_ref[0:1, cols].astype(F32)
    u_prev = jnp.where(i > 0, u_prev, 0.0)
    u_next = jnp.where(i < ni - 1, u_next, 0.0)
    row = lax.broadcasted_iota(jnp.int32, u.shape, 0)
    up = jnp.where(row == 0, u_prev, pltpu.roll(u, 1, 0))
    dn = jnp.where(row == tm - 1, u_next, pltpu.roll(u, tm - 1, 0))
    c = up * w_ref[0:1, cols] + u * w_ref[1:2, cols] + dn * w_ref[2:3, cols] + b_ref[:, cols]
    half = 0.5 * gt_ref[:, cols].astype(F32)
    return bg_ref[:, cols].astype(F32) * c * (half + half * jnp.tanh(half))


def _conv_kernel(*refs):
    o_ref = refs[-1]
    width = o_ref.shape[1] // CONV_COL_CHUNKS
    for k in range(CONV_COL_CHUNKS):
        cols = slice(k * width, (k + 1) * width)
        o_ref[:, cols] = _conv_block(pl.program_id(0), pl.num_programs(0), cols,
                                     *refs[:-1]).astype(o_ref.dtype)


def _short_conv(p_rest, conv_w, conv_b, d, col0):
    s = p_rest.shape[0]
    halo = BF16_SUBLANES
    tm = _tile(s, CONV_ROWS, halo)
    tc = _tile(math.gcd(d, col0), CONV_COLS, CONV_COL_CHUNKS * LANES)
    nc = d // tc
    hb = tm // halo
    last_hb = s // halo - 1

    def cur(k):
        return pl.BlockSpec((tm, tc), lambda i, c: (i, (col0 // tc) + k * nc + c))

    def prev(k):
        return pl.BlockSpec((halo, tc),
                            lambda i, c: (jnp.maximum(i * hb - 1, 0), (col0 // tc) + k * nc + c))

    def nxt(k):
        return pl.BlockSpec((halo, tc),
                            lambda i, c: (jnp.minimum((i + 1) * hb, last_hb), (col0 // tc) + k * nc + c))

    return pl.pallas_call(
        _conv_kernel,
        grid=(s // tm, nc),
        in_specs=[cur(0), cur(1), cur(2), cur(3), prev(1), prev(2), nxt(1), nxt(2),
                  pl.BlockSpec((3, tc), lambda i, c: (0, c)),
                  pl.BlockSpec((1, tc), lambda i, c: (0, c))],
        out_specs=pl.BlockSpec((tm, tc), lambda i, c: (i, c)),
        out_shape=jax.ShapeDtypeStruct((s, d), BF16),
        compiler_params=_params("parallel", "parallel"),
        name="short_conv",
    )(p_rest, p_rest, p_rest, p_rest, p_rest, p_rest, p_rest, p_rest, conv_w, conv_b.reshape(1, d))


def _merge_kernel(za_ref, zb_ref, wa_ref, wb_ref, la_ref, lb_ref, ba_ref, bb_ref, o_ref):
    ya = jnp.dot(za_ref[...], wa_ref[...], preferred_element_type=F32)
    yb = jnp.dot(zb_ref[...], wb_ref[...], preferred_element_type=F32)
    ga = jax.nn.sigmoid(la_ref[...].astype(F32) + ba_ref[...])
    gb = jax.nn.sigmoid(lb_ref[...].astype(F32) + bb_ref[...])
    o_ref[...] = (ga * ya + gb * yb).astype(o_ref.dtype)


def _out_merge(za, zb, wa, wb, p_rest, b_merge, col_a, col_b):
    s, d = za.shape
    tm = _tile(s, MERGE_ROWS, 8)
    tn = _tile(math.gcd(d, col_a), MERGE_COLS, LANES)
    nn = d // tn
    return pl.pallas_call(
        _merge_kernel,
        grid=(s // tm, nn),
        in_specs=[pl.BlockSpec((tm, d), lambda i, j: (i, 0)),
                  pl.BlockSpec((tm, d), lambda i, j: (i, 0)),
                  pl.BlockSpec((d, tn), lambda i, j: (0, j)),
                  pl.BlockSpec((d, tn), lambda i, j: (0, j)),
                  pl.BlockSpec((tm, tn), lambda i, j: (i, col_a // tn + j)),
                  pl.BlockSpec((tm, tn), lambda i, j: (i, col_b // tn + j)),
                  pl.BlockSpec((1, tn), lambda i, j: (0, j)),
                  pl.BlockSpec((1, tn), lambda i, j: (0, nn + j))],
        out_specs=pl.BlockSpec((tm, tn), lambda i, j: (i, j)),
        out_shape=jax.ShapeDtypeStruct((s, d), BF16),
        compiler_params=_params("parallel", "parallel"),
        name="out_proj_merge",
    )(za, zb, wa, wb, p_rest, p_rest, b_merge, b_merge)


def _final_kernel(m_ref, w_ref, x_ref, g_ref, o_ref):
    half = m_ref.shape[0] // 2
    for r in range(2):
        rows = slice(r * half, (r + 1) * half)
        o = jnp.dot(m_ref[rows, :], w_ref[...], preferred_element_type=F32)
        ms = jnp.mean(o * o, axis=-1, keepdims=True)
        o_ref[rows, :] = x_ref[rows, :] + o * lax.rsqrt(ms + RMS_EPS) * g_ref[...]


def _final(m, wo, x, gain):
    s, d = m.shape
    tm = _tile(s, FINAL_ROWS, 8)
    return pl.pallas_call(
        _final_kernel,
        grid=(s // tm,),
        in_specs=[pl.BlockSpec((tm, d), lambda i: (i, 0)),
                  pl.BlockSpec((d, d), lambda i: (0, 0), pipeline_mode=pl.Buffered(1)),
                  pl.BlockSpec((tm, d), lambda i: (i, 0)),
                  pl.BlockSpec((1, d), lambda i: (0, 0))],
        out_specs=pl.BlockSpec((tm, d), lambda i: (i, 0)),
        out_shape=jax.ShapeDtypeStruct((s, d), F32),
        compiler_params=_params("parallel"),
        name="wo_norm_residual",
    )(m, wo, x, gain.reshape(1, d))


def _rotary_tables(s):
    pos = jnp.arange(s, dtype=F32)
    inv_freq = ROPE_THETA ** (-jnp.arange(0, ROT_DIM, 2, dtype=F32) / ROT_DIM)
    ang = pos[:, None] * inv_freq[None, :]
    tab = jnp.concatenate([jnp.cos(ang), jnp.sin(ang), jnp.zeros((s, HEAD_DIM - ROT_DIM), F32)], axis=1)
    return jnp.pad(tab, ((BLOCK, BLOCK), (0, 0)))


def _layer(x, norm_pre, w_in, b_merge, sink, conv_w, conv_b, wa, wb, wo, norm_post):
    s, d = x.shape
    n_q = d // HEAD_DIM
    n_kv = max(n_q // GROUP, 1)
    attn_w, kv_w = n_q * HEAD_DIM, n_kv * HEAD_DIM
    qkvg_cols = 2 * attn_w + 2 * kv_w
    rest_cols = w_in.shape[1] - qkvg_cols
    merge_a_col, merge_b_col = 4 * d, 5 * d

    h = _rmsnorm(x, norm_pre)
    assert kv_w <= PROJ_COLS and qkvg_cols % kv_w == 0, (kv_w, qkvg_cols)
    qkvg, wa16, wb16, wo16 = _in_proj(h, w_in, 0, qkvg_cols, kv_w, True, [wa, wb, wo], None,
                                      "in_proj_qkvg")
    tn_b, plan = _plan_rest(rest_cols, s // _tile(s, PROJ_ROWS, 8), n_kv, s // BLOCK)
    p_rest, za = _in_proj(h, w_in, qkvg_cols, rest_cols, tn_b, False, [],
                          (plan, qkvg, _rotary_tables(s), sink, n_q, n_kv), "in_proj_rest_attn")
    zb = _short_conv(p_rest, conv_w, conv_b, d, 0)
    m = _out_merge(za, zb, wa16, wb16, p_rest, b_merge.reshape(1, 2 * d), merge_a_col, merge_b_col)
    return _final(m, wo16, x, norm_post)


@jax.jit
def kernel(x, norm_pre, w_in, b_merge, attn_sink, conv_w, conv_b, w_attn_out, w_conv_out, w_out, norm_post):
    b, s, d = x.shape
    depth = norm_pre.shape[0]
    outs = []
    for bi in range(b):
        xb = x.reshape(s, d) if b == 1 else x[bi]
        for l in range(depth):
            xb = _layer(xb, norm_pre[l], w_in[l], b_merge[l], attn_sink[l], conv_w[l], conv_b[l],
                        w_attn_out[l], w_conv_out[l], w_out[l], norm_post[l])
        outs.append(xb)
    return outs[0].reshape(1, s, d) if b == 1 else jnp.stack(outs, axis=0)
```

```python
import functools
import math
from typing import NamedTuple, Optional

import jax
import jax.numpy as jnp
from jax import lax
from jax.experimental import pallas as pl
from jax.experimental.pallas import tpu as pltpu

HEAD_DIM = 128
GROUP = 4
WINDOW = 128
BLOCK = 128
ROPE_THETA = 500000.0
ROT_DIM = HEAD_DIM // 4
ROT_HALF = ROT_DIM // 2
RMS_EPS = 1e-6
LOG2E = 1.4426950408889634
LANES = 128
BF16_SUBLANES = 16
V7X_VMEM_BYTES = 64 * 1024 * 1024
VMEM_LIMIT_BYTES = V7X_VMEM_BYTES - 6 * 1024 * 1024

PROJ_ROWS, PROJ_COLS = 1024, 1024
MERGE_ROWS, MERGE_COLS = 1024, 512
FINAL_ROWS = 256
NORM_ROWS = 512
CONV_ROWS, CONV_COLS = 512, 4096

F32 = jnp.float32
BF16 = jnp.bfloat16


def _tile(dim, pref, unit):
    t = min(pref, dim)
    t -= t % unit
    while t > unit and dim % t:
        t -= unit
    assert t >= unit and dim % t == 0, (dim, pref, unit)
    return t


def _params(*sem):
    return pltpu.CompilerParams(dimension_semantics=sem, vmem_limit_bytes=VMEM_LIMIT_BYTES)


def _rmsnorm_kernel(x_ref, g_ref, o_ref):
    x = x_ref[...]
    ms = jnp.mean(x * x, axis=-1, keepdims=True)
    o_ref[...] = (x * lax.rsqrt(ms + RMS_EPS) * g_ref[...]).astype(o_ref.dtype)


def _rmsnorm(x, gain):
    s, d = x.shape
    tm = _tile(s, NORM_ROWS, 8)
    return pl.pallas_call(
        _rmsnorm_kernel,
        grid=(s // tm,),
        in_specs=[pl.BlockSpec((tm, d), lambda i: (i, 0)),
                  pl.BlockSpec((1, d), lambda i: (0, 0))],
        out_specs=pl.BlockSpec((tm, d), lambda i: (i, 0)),
        out_shape=jax.ShapeDtypeStruct((s, d), BF16),
        compiler_params=_params("parallel"),
        name="rmsnorm_pre",
    )(x, gain.reshape(1, d))


class _AttnPlan(NamedTuple):
    nq: int
    units_per_head: int
    units: int


def _rotary_lanes(p):
    lane = lax.broadcasted_iota(jnp.int32, p.shape, 1)
    first, second = lane < ROT_HALF, (lane >= ROT_HALF) & (lane < ROT_DIM)
    c = jnp.where(first, p, jnp.where(second, pltpu.roll(p, ROT_HALF, 1), 1.0))
    s1 = jnp.where(first, -pltpu.roll(p, HEAD_DIM - ROT_HALF, 1), 0.0)
    s2 = jnp.where(second, p, 0.0)
    return c, s1, s2


def _rotate(t, tabs):
    c, s1, s2 = tabs
    return t * c + pltpu.roll(t, HEAD_DIM - ROT_HALF, 1) * s1 + pltpu.roll(t, ROT_HALF, 1) * s2


def _attn_unit(nq, hkv, t, nt, sink_ref, q_ref, kvp_ref, kvm_ref, kvn_ref,
               gate_ref, tab_ref, o_ref, s_scr, p_scr):
    rows = GROUP * BLOCK
    scale = HEAD_DIM ** -0.5
    kfull = jnp.concatenate([kvp_ref[0, 0], kvm_ref[0, 0], kvn_ref[0, 0]], axis=0)
    tabs = _rotary_lanes(tab_ref[...])
    tab_m = [x[BLOCK:(nq + 1) * BLOCK] for x in tabs]
    kfull = _rotate(kfull.astype(F32), tabs).astype(kfull.dtype)
    vfull = jnp.concatenate([kvp_ref[1, 0], kvm_ref[1, 0], kvn_ref[1, 0]], axis=0)
    vext = jnp.concatenate([vfull, jnp.ones_like(vfull)], axis=1)

    for b in range(nq):
        q4 = q_ref[:, b * BLOCK:(b + 1) * BLOCK, :].reshape(rows, HEAD_DIM)
        tab_q = [jnp.concatenate([x[b * BLOCK:(b + 1) * BLOCK]] * GROUP, axis=0) for x in tab_m]
        q4 = _rotate(q4.astype(F32), tab_q).astype(q4.dtype)
        s_scr[b] = lax.dot_general(q4, kfull[b * BLOCK:(b + 3) * BLOCK], (((1,), (1,)), ((), ())),
                                   preferred_element_type=F32)

    qi = lax.broadcasted_iota(jnp.int32, (rows, BLOCK), 0) % BLOCK
    kc = lax.broadcasted_iota(jnp.int32, (rows, BLOCK), 1)
    no_prev = jnp.where(t > 0, 0, BLOCK)
    no_next = jnp.where(t < nt - 1, 0, BLOCK)
    sink_col = jnp.concatenate(
        [jnp.full((BLOCK, 1), sink_ref[hkv * GROUP + g], F32) for g in range(GROUP)], axis=0)
    sink_terms = []
    for b in range(nq):
        ok_prev = kc >= (qi + no_prev if b == 0 else qi)
        ok_next = kc <= (qi - no_next if b == nq - 1 else qi)
        s0 = jnp.where(ok_prev, s_scr[b, :, 0:BLOCK], -jnp.inf)
        s1 = s_scr[b, :, BLOCK:2 * BLOCK]
        s2 = jnp.where(ok_next, s_scr[b, :, 2 * BLOCK:3 * BLOCK], -jnp.inf)
        m_raw = jnp.max(jnp.maximum(jnp.maximum(s0, s1), s2), axis=-1, keepdims=True)
        m = jnp.maximum(m_raw * scale, sink_col)
        m2 = m * LOG2E
        for k, sk in enumerate((s0, s1, s2)):
            p_scr[b, :, k * BLOCK:(k + 1) * BLOCK] = jnp.exp2(sk * (scale * LOG2E) - m2).astype(p_scr.dtype)
        sink_terms.append(jnp.exp(sink_col - m))

    for b in range(nq):
        o2 = jnp.dot(p_scr[b], vext[b * BLOCK:(b + 3) * BLOCK], preferred_element_type=F32)
        num = o2[:, :HEAD_DIM]
        den = o2[:, HEAD_DIM:] + sink_terms[b]
        for g in range(GROUP):
            gate = gate_ref[g, b * BLOCK:(b + 1) * BLOCK, :].astype(F32)
            r = slice(g * BLOCK, (g + 1) * BLOCK)
            o_ref[b * BLOCK:(b + 1) * BLOCK, g * HEAD_DIM:(g + 1) * HEAD_DIM] = (
                (num[r] * gate) / (den[r] * (1.0 + jnp.exp(-gate)))).astype(o_ref.dtype)


def _attn_specs(plan, n_i, n_q, n_kv, nb):
    nq, upk, units = plan
    tq = nq * BLOCK
    assert (n_q + 2 * n_kv) % GROUP == 0, "attn_gate heads must start on a GROUP boundary"
    g0 = (n_q + 2 * n_kv) // GROUP
    kv_pair = GROUP // 2

    def unit(j, i):
        u = jnp.minimum(j * n_i + i, units - 1)
        return lax.div(u, upk), lax.rem(u, upk)

    def at(fn):
        return lambda j, i: fn(*unit(j, i))

    in_specs = [pl.BlockSpec(memory_space=pltpu.SMEM),
                pl.BlockSpec((GROUP, tq, HEAD_DIM), at(lambda h, t: (h, t, 0))),
                pl.BlockSpec((2, 1, BLOCK, HEAD_DIM),
                             at(lambda h, t: (kv_pair, h, jnp.maximum(t * nq - 1, 0), 0))),
                pl.BlockSpec((2, 1, tq, HEAD_DIM), at(lambda h, t: (kv_pair, h, t, 0))),
                pl.BlockSpec((2, 1, BLOCK, HEAD_DIM),
                             at(lambda h, t: (kv_pair, h, jnp.minimum((t + 1) * nq, nb - 1), 0))),
                pl.BlockSpec((GROUP, tq, HEAD_DIM), at(lambda h, t: (g0 + h, t, 0))),
                pl.BlockSpec((pl.Element(tq + 2 * BLOCK), pl.Element(HEAD_DIM)),
                             at(lambda h, t: (t * tq, 0)))]
    out_spec = pl.BlockSpec((tq, GROUP * HEAD_DIM), at(lambda h, t: (t, h)))
    scratch = [pltpu.VMEM((nq, GROUP * BLOCK, 3 * BLOCK), F32),
               pltpu.VMEM((nq, GROUP * BLOCK, 3 * BLOCK), BF16)]
    return in_specs, out_spec, scratch


class _ProjCfg(NamedTuple):
    head_major: bool
    n_riders: int
    kc: int
    col0: int
    attn: Optional[_AttnPlan]


N_ATTN_IN = 7


def _in_proj_kernel(cfg, h_ref, w_hbm, *refs):
    refs = list(refs)
    take = lambda n: [refs.pop(0) for _ in range(n)]
    riders_in = take(cfg.n_riders)
    attn_in = take(N_ATTN_IN if cfg.attn else 0)
    o_ref, = take(1)
    riders_out = take(cfg.n_riders)
    z_ref = take(1 if cfg.attn else 0)
    w_even, w_odd, stage, sem = take(4)
    attn_scratch = refs
    kc = cfg.kc
    j, i = pl.program_id(0), pl.program_id(1)
    n_j, n_i = pl.num_programs(0), pl.num_programs(1)
    tn = w_even.shape[1]
    step = j * n_i + i
    slot = step % 2

    def chunk_copy(jb, ic, to_slot):
        return pltpu.make_async_copy(
            w_hbm.at[pl.ds(ic * kc, kc), pl.ds(cfg.col0 + jb * tn, tn)],
            stage.at[to_slot], sem.at[to_slot])

    j_next = jnp.minimum(j + 1, n_j - 1)

    @pl.when(step == 0)
    def _():
        n_chunks = w_even.shape[0] // kc
        for ic in range(min(2, n_chunks)):
            chunk_copy(0, ic, ic).start()
        for ic in range(n_chunks):
            chunk_copy(0, ic, ic % 2).wait()
            w_even[ic * kc:(ic + 1) * kc, :] = stage[ic % 2].astype(BF16)
            if ic + 2 < n_chunks:
                chunk_copy(0, ic + 2, ic % 2).start()
        chunk_copy(j_next, 0, 0).start(priority=1)

    @pl.when(step + 1 < n_j * n_i)
    def _():
        wrap = i + 1 == n_i
        chunk_copy(jnp.minimum(jnp.where(wrap, j + 2, j + 1), n_j - 1),
                   jnp.where(wrap, 0, i + 1), 1 - slot).start(priority=1)

    chunk_copy(j_next, i, slot).wait()

    def compute(w_cur, w_next):
        for src, dst in zip(riders_in, riders_out):
            dst[...] = src[...].astype(dst.dtype)
        acc = jnp.dot(h_ref[...], w_cur[...], preferred_element_type=F32)
        if cfg.head_major:
            for g in range(o_ref.shape[0]):
                o_ref[g] = acc[:, g * HEAD_DIM:(g + 1) * HEAD_DIM].astype(o_ref.dtype)
        else:
            o_ref[...] = acc.astype(o_ref.dtype)
        w_next[pl.ds(pl.multiple_of(i * kc, kc), kc), :] = stage[slot].astype(BF16)
        if cfg.attn:
            unit = jnp.minimum(step, cfg.attn.units - 1)
            upk = cfg.attn.units_per_head
            _attn_unit(cfg.attn.nq, lax.div(unit, upk), lax.rem(unit, upk), upk,
                       *attn_in, *z_ref, *attn_scratch)

    pl.when(j % 2 == 0)(lambda: compute(w_even, w_odd))
    pl.when(j % 2 == 1)(lambda: compute(w_odd, w_even))


def _plan_rest(n_cols, n_i, n_kv, nb):
    best = None
    for tn in range(PROJ_COLS, 0, -LANES):
        if n_cols % tn:
            continue
        steps = (n_cols // tn) * n_i
        for nq in (1, 2, 4, 8):
            if nb % nq or n_kv * (nb // nq) > steps:
                continue
            waste = steps - n_kv * (nb // nq)
            if best is None or waste < best[0]:
                best = (waste, tn, _AttnPlan(nq, nb // nq, n_kv * (nb // nq)))
            break
    assert best is not None, (n_cols, n_i, n_kv, nb)
    return best[1], best[2]


def _in_proj(h, w_in, col0, n_cols, tn, head_major, riders, attn, name):
    s, d = h.shape
    tm = _tile(s, PROJ_ROWS, 8)
    n_j, n_i = n_cols // tn, s // tm
    kc = d // n_i
    assert n_cols % tn == 0 and d % n_i == 0 and kc % BF16_SUBLANES == 0, (n_cols, tn, d, n_i)

    in_specs = [pl.BlockSpec((tm, d), lambda j, i: (i, 0)),
                pl.BlockSpec(memory_space=pl.ANY)]
    args = [h, w_in]
    if head_major:
        assert tn % HEAD_DIM == 0
        out_specs = [pl.BlockSpec((tn // HEAD_DIM, tm, HEAD_DIM), lambda j, i: (j, i, 0))]
        out_shape = [jax.ShapeDtypeStruct((n_cols // HEAD_DIM, s, HEAD_DIM), BF16)]
    else:
        out_specs = [pl.BlockSpec((tm, tn), lambda j, i: (i, j))]
        out_shape = [jax.ShapeDtypeStruct((s, n_cols), BF16)]

    for r in riders:
        rows, cols = r.shape
        rr = next(c for c in range(BF16_SUBLANES, rows + 1, BF16_SUBLANES)
                  if rows % c == 0 and rows // c <= n_j * n_i)
        spec = pl.BlockSpec((rr, cols), functools.partial(
            lambda j, i, last: (jnp.minimum(j * n_i + i, last), 0), last=rows // rr - 1))
        in_specs.append(spec)
        args.append(r)
        out_specs.append(spec)
        out_shape.append(jax.ShapeDtypeStruct(r.shape, BF16))

    scratch = [pltpu.VMEM((d, tn), BF16), pltpu.VMEM((d, tn), BF16),
               pltpu.VMEM((2, kc, tn), F32), pltpu.SemaphoreType.DMA((2,))]
    plan = None
    if attn is not None:
        plan, qkvg, tabs, sink, n_q, n_kv = attn
        a_in, a_out, a_scratch = _attn_specs(plan, n_i, n_q, n_kv, s // BLOCK)
        in_specs += a_in
        kv4 = qkvg.reshape(-1, n_kv, s, HEAD_DIM)
        args += [sink, qkvg, kv4, kv4, kv4, qkvg, tabs]
        out_specs.append(a_out)
        out_shape.append(jax.ShapeDtypeStruct((s, n_q * HEAD_DIM), BF16))
        scratch += a_scratch

    cfg = _ProjCfg(head_major, len(riders), kc, col0, plan)
    return pl.pallas_call(
        functools.partial(_in_proj_kernel, cfg),
        grid=(n_j, n_i),
        in_specs=in_specs,
        out_specs=out_specs,
        out_shape=out_shape,
        scratch_shapes=scratch,
        compiler_params=_params("arbitrary", "arbitrary"),
        name=name,
    )(*args)


CONV_CHUNK_COLS = 256


def _conv_block(i, ni, cols, bg_ref, cg_ref, cx_ref, gt_ref, cgp_ref, cxp_ref, cgn_ref, cxn_ref,
                w_ref, b_ref):
    tm = bg_ref.shape[0]
    halo = cgp_ref.shape[0]
    u = cg_ref[:, cols].astype(F32) * cx_ref[:, cols].astype(F32)
    u_prev = cgp_ref[halo - 1:halo, cols].astype(F32) * cxp_ref[halo - 1:halo, cols].astype(F32)
    u_next = cgn_ref[0:1, cols].astype(F32) * cxn_ref[0:1, cols].astype(F32)
    u_prev = jnp.where(i > 0, u_prev, 0.0)
    u_next = jnp.where(i < ni - 1, u_next, 0.0)
    row = lax.broadcasted_iota(jnp.int32, u.shape, 0)
    up = jnp.where(row == 0, u_prev, pltpu.roll(u, 1, 0))
    dn = jnp.where(row == tm - 1, u_next, pltpu.roll(u, tm - 1, 0))
    c = up * w_ref[0:1, cols] + u * w_ref[1:2, cols] + dn * w_ref[2:3, cols] + b_ref[:, cols]
    half = 0.5 * gt_ref[:, cols].astype(F32)
    return bg_ref[:, cols].astype(F32) * c * (half + half * jnp.tanh(half))


def _conv_kernel(*refs):
    o_ref = refs[-1]
    for k in range(o_ref.shape[1] // CONV_CHUNK_COLS):
        cols = slice(k * CONV_CHUNK_COLS, (k + 1) * CONV_CHUNK_COLS)
        o_ref[:, cols] = _conv_block(pl.program_id(0), pl.num_programs(0), cols,
                                     *refs[:-1]).astype(o_ref.dtype)


def _short_conv(p_rest, conv_w, conv_b, d, col0):
    s = p_rest.shape[0]
    halo = BF16_SUBLANES
    tm = _tile(s, CONV_ROWS, halo)
    tc = _tile(math.gcd(d, col0), CONV_COLS, CONV_CHUNK_COLS)
    nc = d // tc
    hb = tm // halo
    last_hb = s // halo - 1

    def cur(k):
        return pl.BlockSpec((tm, tc), lambda i, c: (i, (col0 // tc) + k * nc + c))

    def prev(k):
        return pl.BlockSpec((halo, tc),
                            lambda i, c: (jnp.maximum(i * hb - 1, 0), (col0 // tc) + k * nc + c))

    def nxt(k):
        return pl.BlockSpec((halo, tc),
                            lambda i, c: (jnp.minimum((i + 1) * hb, last_hb), (col0 // tc) + k * nc + c))

    return pl.pallas_call(
        _conv_kernel,
        grid=(s // tm, nc),
        in_specs=[cur(0), cur(1), cur(2), cur(3), prev(1), prev(2), nxt(1), nxt(2),
                  pl.BlockSpec((3, tc), lambda i, c: (0, c)),
                  pl.BlockSpec((1, tc), lambda i, c: (0, c))],
        out_specs=pl.BlockSpec((tm, tc), lambda i, c: (i, c)),
        out_shape=jax.ShapeDtypeStruct((s, d), BF16),
        compiler_params=_params("parallel", "parallel"),
        name="short_conv",
    )(p_rest, p_rest, p_rest, p_rest, p_rest, p_rest, p_rest, p_rest, conv_w, conv_b.reshape(1, d))


def _merge_kernel(za_ref, zb_ref, wa_ref, wb_ref, la_ref, lb_ref, ba_ref, bb_ref, o_ref):
    ya = jnp.dot(za_ref[...], wa_ref[...], preferred_element_type=F32)
    yb = jnp.dot(zb_ref[...], wb_ref[...], preferred_element_type=F32)
    ga = jax.nn.sigmoid(la_ref[...].astype(F32) + ba_ref[...])
    gb = jax.nn.sigmoid(lb_ref[...].astype(F32) + bb_ref[...])
    o_ref[...] = (ga * ya + gb * yb).astype(o_ref.dtype)


def _out_merge(za, zb, wa, wb, p_rest, b_merge, col_a, col_b):
    s, d = za.shape
    tm = _tile(s, MERGE_ROWS, 8)
    tn = _tile(math.gcd(d, col_a), MERGE_COLS, LANES)
    nn = d // tn
    return pl.pallas_call(
        _merge_kernel,
        grid=(s // tm, nn),
        in_specs=[pl.BlockSpec((tm, d), lambda i, j: (i, 0)),
                  pl.BlockSpec((tm, d), lambda i, j: (i, 0)),
                  pl.BlockSpec((d, tn), lambda i, j: (0, j)),
                  pl.BlockSpec((d, tn), lambda i, j: (0, j)),
                  pl.BlockSpec((tm, tn), lambda i, j: (i, col_a // tn + j)),
                  pl.BlockSpec((tm, tn), lambda i, j: (i, col_b // tn + j)),
                  pl.BlockSpec((1, tn), lambda i, j: (0, j)),
                  pl.BlockSpec((1, tn), lambda i, j: (0, nn + j))],
        out_specs=pl.BlockSpec((tm, tn), lambda i, j: (i, j)),
        out_shape=jax.ShapeDtypeStruct((s, d), BF16),
        compiler_params=_params("parallel", "parallel"),
        name="out_proj_merge",
    )(za, zb, wa, wb, p_rest, p_rest, b_merge, b_merge)


def _final_kernel(m_ref, w_ref, x_ref, g_ref, o_ref):
    o = jnp.dot(m_ref[...], w_ref[...], preferred_element_type=F32)
    ms = jnp.mean(o * o, axis=-1, keepdims=True)
    o_ref[...] = x_ref[...] + o * lax.rsqrt(ms + RMS_EPS) * g_ref[...]


def _final(m, wo, x, gain):
    s, d = m.shape
    tm = _tile(s, FINAL_ROWS, 8)
    return pl.pallas_call(
        _final_kernel,
        grid=(s // tm,),
        in_specs=[pl.BlockSpec((tm, d), lambda i: (i, 0)),
                  pl.BlockSpec((d, d), lambda i: (0, 0), pipeline_mode=pl.Buffered(1)),
                  pl.BlockSpec((tm, d), lambda i: (i, 0)),
                  pl.BlockSpec((1, d), lambda i: (0, 0))],
        out_specs=pl.BlockSpec((tm, d), lambda i: (i, 0)),
        out_shape=jax.ShapeDtypeStruct((s, d), F32),
        compiler_params=_params("parallel"),
        name="wo_norm_residual",
    )(m, wo, x, gain.reshape(1, d))


def _rotary_tables(s):
    pos = jnp.arange(s, dtype=F32)
    inv_freq = ROPE_THETA ** (-jnp.arange(0, ROT_DIM, 2, dtype=F32) / ROT_DIM)
    ang = pos[:, None] * inv_freq[None, :]
    tab = jnp.concatenate([jnp.cos(ang), jnp.sin(ang), jnp.zeros((s, HEAD_DIM - ROT_DIM), F32)], axis=1)
    return jnp.pad(tab, ((BLOCK, BLOCK), (0, 0)))


def _layer(x, norm_pre, w_in, b_merge, sink, conv_w, conv_b, wa, wb, wo, norm_post):
    s, d = x.shape
    n_q = d // HEAD_DIM
    n_kv = max(n_q // GROUP, 1)
    attn_w, kv_w = n_q * HEAD_DIM, n_kv * HEAD_DIM
    qkvg_cols = 2 * attn_w + 2 * kv_w
    rest_cols = w_in.shape[1] - qkvg_cols
    merge_a_col, merge_b_col = 4 * d, 5 * d

    h = _rmsnorm(x, norm_pre)
    assert kv_w <= PROJ_COLS and qkvg_cols % kv_w == 0, (kv_w, qkvg_cols)
    qkvg, wa16, wb16, wo16 = _in_proj(h, w_in, 0, qkvg_cols, kv_w, True, [wa, wb, wo], None,
                                      "in_proj_qkvg")
    tn_b, plan = _plan_rest(rest_cols, s // _tile(s, PROJ_ROWS, 8), n_kv, s // BLOCK)
    p_rest, za = _in_proj(h, w_in, qkvg_cols, rest_cols, tn_b, False, [],
                          (plan, qkvg, _rotary_tables(s), sink, n_q, n_kv), "in_proj_rest_attn")
    zb = _short_conv(p_rest, conv_w, conv_b, d, 0)
    m = _out_merge(za, zb, wa16, wb16, p_rest, b_merge.reshape(1, 2 * d), merge_a_col, merge_b_col)
    return _final(m, wo16, x, norm_post)


@jax.jit
def kernel(x, norm_pre, w_in, b_merge, attn_sink, conv_w, conv_b, w_attn_out, w_conv_out, w_out, norm_post):
    b, s, d = x.shape
    depth = norm_pre.shape[0]
    outs = []
    for bi in range(b):
        xb = x.reshape(s, d) if b == 1 else x[bi]
        for l in range(depth):
            xb = _layer(xb, norm_pre[l], w_in[l], b_merge[l], attn_sink[l], conv_w[l], conv_b[l],
                        w_attn_out[l], w_conv_out[l], w_out[l], norm_post[l])
        outs.append(xb)
    return outs[0].reshape(1, s, d) if b == 1 else jnp.stack(outs, axis=0)
```
